```python
import math
import jax, jax.numpy as jnp
from jax import lax
import numpy as np

D_MODEL = 4096
BATCH = 1
SEQ = 16384
DEPTH = 1

N_HEADS = 16
HEAD_DIM = 64
ATTN_QK = N_HEADS * 2 * HEAD_DIM
ATTN_V = N_HEADS * 2 * HEAD_DIM
Q_BLOCK = 128
D_RNN = D_MODEL // 2
N_RNN_BLOCKS = 16
RNN_BLOCK = D_RNN // N_RNN_BLOCKS
CONV_RNN = 4
RG_C = 8.0
D_FF = 11008
CONV_FFN = 3
EPS = 1e-6
D_IN = 2 * ATTN_QK + ATTN_V + 2 * D_RNN + 2 * D_MODEL
IN_SPLITS = (ATTN_QK, 2 * ATTN_QK, 2 * ATTN_QK + ATTN_V,
             2 * ATTN_QK + ATTN_V + D_RNN, 2 * ATTN_QK + ATTN_V + 2 * D_RNN,
             2 * ATTN_QK + ATTN_V + 2 * D_RNN + D_MODEL)

kernel_name = "hybrid_diffattn_rglru_convffn_adaln"


def rmsnorm(x, w):
    xf = x.astype(jnp.float32)
    y = xf * lax.rsqrt(jnp.mean(xf * xf, axis=-1, keepdims=True) + EPS)
    return (y * w.astype(jnp.float32)).astype(x.dtype)


def modulate(h, shift, scale):
    return h * (1.0 + scale[:, None, :]) + shift[:, None, :]


def causal_dwconv(x, w, b):
    K = w.shape[0]
    S = x.shape[1]
    xp = jnp.pad(x, ((0, 0), (K - 1, 0), (0, 0)))
    y = b
    for j in range(K):
        y = y + xp[:, j:j + S] * w[j]
    return y


def lambda_init_fn(layer_idx):
    return 0.8 - 0.6 * math.exp(-0.3 * layer_idx)


def diff_attention(q, k, v, lq1, lk1, lq2, lk2, subln_w, lambda_init):
    B, S = q.shape[0], q.shape[1]
    q = q.reshape(B, S, N_HEADS, 2, HEAD_DIM)
    k = k.reshape(B, S, N_HEADS, 2, HEAD_DIM)
    v = v.reshape(B, S, N_HEADS, 2 * HEAD_DIM)
    f32 = jnp.float32
    lam = (jnp.exp(jnp.sum(lq1.astype(f32) * lk1.astype(f32)))
           - jnp.exp(jnp.sum(lq2.astype(f32) * lk2.astype(f32))) + lambda_init)
    slopes = 2.0 ** (-8.0 * jnp.arange(1, N_HEADS + 1, dtype=f32) / N_HEADS)
    kpos = jnp.arange(S, dtype=f32)
    n_blk = S // Q_BLOCK
    qb = q.reshape(B, n_blk, Q_BLOCK, N_HEADS, 2, HEAD_DIM).transpose(1, 0, 2, 3, 4, 5)
    starts = jnp.arange(n_blk, dtype=jnp.int32) * Q_BLOCK
    scale = HEAD_DIM ** -0.5

    def block(args):
        q_blk, start = args
        s = jnp.einsum('bqhcd,bkhcd->bhcqk', q_blk, k).astype(f32) * scale
        qpos = start.astype(f32) + jnp.arange(Q_BLOCK, dtype=f32)
        dist = qpos[:, None] - kpos[None, :]
        bias = jnp.where(dist >= 0, -slopes[:, None, None] * dist, -jnp.inf)
        p = jax.nn.softmax(s + bias[None, :, None], axis=-1)
        w = p[:, :, 0] - lam * p[:, :, 1]
        return jnp.einsum('bhqk,bkhe->bqhe', w.astype(v.dtype), v)

    o = lax.map(block, (qb, starts))
    o = o.transpose(1, 0, 2, 3, 4).reshape(B, S, N_HEADS, 2 * HEAD_DIM)
    o = rmsnorm(o, subln_w) * (1.0 - lambda_init)
    return o.reshape(B, S, N_HEADS * 2 * HEAD_DIM)


def rg_lru(x, w_a, b_a, w_x, b_x, lru_lambda):
    B, S = x.shape[0], x.shape[1]
    xb = x.reshape(B, S, N_RNN_BLOCKS, RNN_BLOCK)
    r = jax.nn.sigmoid(jnp.einsum('bsnd,nde->bsne', xb, w_a).reshape(B, S, D_RNN) + b_a)
    i = jax.nn.sigmoid(jnp.einsum('bsnd,nde->bsne', xb, w_x).reshape(B, S, D_RNN) + b_x)
    f32 = jnp.float32
    log_a = -RG_C * r.astype(f32) * jax.nn.softplus(-lru_lambda.astype(f32))
    a = jnp.exp(log_a)
    bx = jnp.sqrt(-jnp.expm1(2.0 * log_a)) * (i * x).astype(f32)

    def combine(left, right):
        a_l, b_l = left
        a_r, b_r = right
        return a_l * a_r, a_r * b_l + b_r

    _, h = lax.associative_scan(combine, (a, bx), axis=1)
    return h.astype(x.dtype)


def setup_inputs(seed: int = 0) -> dict:
    key = jax.random.key(seed)
    ks = jax.random.split(key, 32)
    f32 = jnp.float32
    D, L = D_MODEL, DEPTH

    def nrm(k, shape, scale):
        return jax.random.normal(k, shape, f32) * scale

    u = jax.random.uniform(ks[17], (L, D_RNN), f32, 0.9, 0.999)
    a0 = u ** (1.0 / RG_C)
    lru_lambda = jnp.log(a0) - jnp.log1p(-a0)
    return {
        "x": nrm(ks[0], (BATCH, SEQ, D), 1.0),
        "c": nrm(ks[1], (BATCH, D), 1.0),
        "w_ada": nrm(ks[2], (L, D, 6 * D), D ** -0.5),
        "b_ada": nrm(ks[3], (L, 6 * D), 0.01),
        "norm1_w": 1.0 + nrm(ks[4], (L, D), 0.01),
        "w_in": nrm(ks[5], (L, D, D_IN), D ** -0.5),
        "lambda_q1": nrm(ks[6], (L, HEAD_DIM), 0.1),
        "lambda_k1": nrm(ks[7], (L, HEAD_DIM), 0.1),
        "lambda_q2": nrm(ks[8], (L, HEAD_DIM), 0.1),
        "lambda_k2": nrm(ks[9], (L, HEAD_DIM), 0.1),
        "subln_w": 1.0 + nrm(ks[10], (L, 2 * HEAD_DIM), 0.01),
        "conv_w": nrm(ks[11], (L, CONV_RNN, D_RNN), CONV_RNN ** -0.5),
        "conv_b": nrm(ks[12], (L, D_RNN), 0.01),
        "w_rg_a": nrm(ks[13], (L, N_RNN_BLOCKS, RNN_BLOCK, RNN_BLOCK), RNN_BLOCK ** -0.5),
        "b_rg_a": nrm(ks[14], (L, D_RNN), 0.01),
        "w_rg_x": nrm(ks[15], (L, N_RNN_BLOCKS, RNN_BLOCK, RNN_BLOCK), RNN_BLOCK ** -0.5),
        "b_rg_x": nrm(ks[16], (L, D_RNN), 0.01),
        "lru_lambda": lru_lambda,
        "w_proj_attn": nrm(ks[18], (L, ATTN_V, D), ATTN_V ** -0.5),
        "w_proj_rec": nrm(ks[19], (L, D_RNN, D), D_RNN ** -0.5),
        "w_out": nrm(ks[20], (L, D, D), D ** -0.5),
        "norm2_w": 1.0 + nrm(ks[21], (L, D), 0.01),
        "w_ffn_up": nrm(ks[22], (L, D, 2 * D_FF), D ** -0.5),
        "ffn_conv_w": nrm(ks[23], (L, CONV_FFN, D_FF), CONV_FFN ** -0.5),
        "ffn_conv_b": nrm(ks[24], (L, D_FF), 0.01),
        "w_ffn_down": nrm(ks[25], (L, D_FF, D), D_FF ** -0.5),
        "norm_f_w": 1.0 + nrm(ks[26], (D,), 0.01),
    }


def reference(x, c, w_ada, b_ada, norm1_w, w_in, lambda_q1, lambda_k1, lambda_q2, lambda_k2,
              subln_w, conv_w, conv_b, w_rg_a, b_rg_a, w_rg_x, b_rg_x, lru_lambda,
              w_proj_attn, w_proj_rec, w_out, norm2_w, w_ffn_up, ffn_conv_w, ffn_conv_b,
              w_ffn_down, norm_f_w):
    for l in range(DEPTH):
        mod = jax.nn.silu(c) @ w_ada[l] + b_ada[l]
        sh1, sc1, g1, sh2, sc2, g2 = jnp.split(mod, 6, axis=-1)

        h = modulate(rmsnorm(x, norm1_w[l]), sh1, sc1)
        proj = h @ w_in[l]
        q, k, v, rx, rg, ga, gb = jnp.split(proj, IN_SPLITS, axis=-1)

        y_att = diff_attention(q, k, v, lambda_q1[l], lambda_k1[l], lambda_q2[l], lambda_k2[l],
                               subln_w[l], lambda_init_fn(l))
        xr = causal_dwconv(rx, conv_w[l], conv_b[l])
        y_rec = rg_lru(xr, w_rg_a[l], b_rg_a[l], w_rg_x[l], b_rg_x[l], lru_lambda[l]) * jax.nn.gelu(rg)

        mixed = (jax.nn.sigmoid(ga) * (y_att @ w_proj_attn[l])
                 + jax.nn.sigmoid(gb) * (y_rec @ w_proj_rec[l]))
        x = x + g1[:, None, :] * (mixed @ w_out[l])

        h = modulate(rmsnorm(x, norm2_w[l]), sh2, sc2)
        gate, up = jnp.split(h @ w_ffn_up[l], 2, axis=-1)
        gate = causal_dwconv(gate, ffn_conv_w[l], ffn_conv_b[l])
        x = x + g2[:, None, :] * ((jax.nn.gelu(gate) * up) @ w_ffn_down[l])

    return rmsnorm(x, norm_f_w)
```

```python
import functools
import math

import jax
import jax.numpy as jnp
from jax import lax
from jax.experimental import pallas as pl
from jax.experimental.pallas import tpu as pltpu

F32 = jnp.float32
BF16 = jnp.bfloat16
EPS = 1e-6
RG_C = 8.0
NEG_BIG = -1e30

V7X_LANES = 128
V7X_SUBLANES = 8
V7X_VMEM_LIMIT_BYTES = 60000 * 1024
MIB = 1024 * 1024


def _cparams(semantics, vmem_bytes):
    return pltpu.CompilerParams(
        dimension_semantics=semantics,
        vmem_limit_bytes=int(min(vmem_bytes, V7X_VMEM_LIMIT_BYTES)),
    )


def _tile(n, want, quantum):
    t = min(n, want) // quantum * quantum
    while t > quantum and n % t:
        t -= quantum
    assert t >= quantum and n % t == 0, (n, want, quantum)
    return t


def _rms_scale(x):
    return lax.rsqrt(jnp.mean(x * x, axis=-1, keepdims=True) + EPS)


def _adaln_mod_kernel(c_ref, w_ref, b_ref, o_ref):
    c = c_ref[...]
    a = c * jax.nn.sigmoid(c)
    a8 = jnp.broadcast_to(a, (V7X_SUBLANES, a.shape[1])).astype(BF16)
    acc = jnp.dot(a8, w_ref[...].astype(BF16), preferred_element_type=F32)
    o_ref[...] = acc[0:1, :] + b_ref[...]


def _adaln_mod(c, w, b):
    d, n = w.shape
    tn = _tile(n, 1024, V7X_LANES)
    return pl.pallas_call(
        _adaln_mod_kernel,
        grid=(n // tn,),
        in_specs=[
            pl.BlockSpec((1, d), lambda j: (0, 0)),
            pl.BlockSpec((d, tn), lambda j: (0, j)),
            pl.BlockSpec((1, tn), lambda j: (0, j)),
        ],
        out_specs=pl.BlockSpec((1, tn), lambda j: (0, j)),
        out_shape=jax.ShapeDtypeStruct((1, n), F32),
        compiler_params=_cparams(("arbitrary",), 2 * d * tn * 4 + d * tn * 2 + 4 * MIB),
        name="adaln_mod",
    )(c, w, b)


def _norm_modulate_kernel(x_ref, w_ref, scale_ref, shift_ref, o_ref):
    x = x_ref[...]
    y = x * _rms_scale(x) * w_ref[...]
    o_ref[...] = (y * (1.0 + scale_ref[...]) + shift_ref[...]).astype(o_ref.dtype)


def _norm_modulate(x, w, mod, shift_idx, scale_idx):
    s, d = x.shape
    tm = _tile(s, 256, 16)
    return pl.pallas_call(
        _norm_modulate_kernel,
        grid=(s // tm,),
        in_specs=[
            pl.BlockSpec((tm, d), lambda i: (i, 0)),
            pl.BlockSpec((1, d), lambda i: (0, 0)),
            pl.BlockSpec((1, d), lambda i: (0, scale_idx)),
            pl.BlockSpec((1, d), lambda i: (0, shift_idx)),
        ],
        out_specs=pl.BlockSpec((tm, d), lambda i: (i, 0)),
        out_shape=jax.ShapeDtypeStruct((s, d), BF16),
        compiler_params=_cparams(("parallel",), 2 * tm * d * 6 + 3 * tm * d * 4 + 2 * MIB),
        name="norm_modulate",
    )(x, w, mod, mod)


def _matmul_kernel(x_ref, w_ref, o_ref):
    o_ref[...] = jnp.dot(x_ref[...], w_ref[...], preferred_element_type=F32).astype(o_ref.dtype)


def _matmul(x, w, out_dtype):
    m, k = x.shape
    _, n = w.shape
    tm = _tile(m, 1024, 16)
    tn = _tile(n, 512, V7X_LANES)
    ob = jnp.dtype(out_dtype).itemsize
    vmem = 2 * (tm * k * 2 + k * tn * 2 + tm * tn * ob) + tm * tn * 4 + 2 * MIB
    return pl.pallas_call(
        _matmul_kernel,
        grid=(m // tm, n // tn),
        in_specs=[
            pl.BlockSpec((tm, k), lambda i, j: (i, 0)),
            pl.BlockSpec((k, tn), lambda i, j: (0, j)),
        ],
        out_specs=pl.BlockSpec((tm, tn), lambda i, j: (i, j)),
        out_shape=jax.ShapeDtypeStruct((m, n), out_dtype),
        compiler_params=_cparams(("parallel", "arbitrary"), vmem),
        name="matmul",
    )(x, w)


def _diff_attn_kernel(q_ref, k_ref, v_ref, cb_ref, sh_ref, lq1_ref, lk1_ref, lq2_ref, lk2_ref, sw_ref,
                      o_ref, m_ref, l_ref, acc_ref, *, tq, tk, dh, scale, lambda_init):
    i = pl.program_id(1)
    q = q_ref[...]
    scale_in_q = math.frexp(scale)[0] == 0.5
    if scale_in_q:
        q = q * jnp.asarray(scale, BF16)
    lane = lax.broadcasted_iota(jnp.int32, q.shape, 1)
    zero = jnp.zeros_like(q)
    q_maps = (jnp.where(lane < dh, q, zero), jnp.where(lane >= dh, q, zero))
    col_bias = cb_ref[0]
    frame_shift = sh_ref[0][:, 0:1]

    m_ref[...] = jnp.full(m_ref.shape, NEG_BIG, F32)
    l_ref[...] = jnp.zeros(l_ref.shape, F32)
    acc_ref[...] = jnp.zeros(acc_ref.shape, F32)

    def chunk(j, mask):
        start = pl.multiple_of(j * tk, tk)
        kj = k_ref[pl.ds(start, tk), :]
        vj = v_ref[pl.ds(start, tk), :]
        for c in range(2):
            s = lax.dot_general(q_maps[c], kj, (((1,), (1,)), ((), ())), preferred_element_type=F32)
            if not scale_in_q:
                s = s * scale
            z = s + col_bias
            if mask is not None:
                z = jnp.where(mask, z, NEG_BIG)
            m_prev = m_ref[c] - frame_shift
            m_new = jnp.maximum(m_prev, jnp.max(z, axis=-1, keepdims=True))
            alpha = jnp.exp(m_prev - m_new)
            p = jnp.exp(z - m_new)
            l_ref[c] = alpha * l_ref[c] + jnp.sum(p, axis=-1, keepdims=True)
            acc_ref[c] = alpha * acc_ref[c] + jnp.dot(p.astype(BF16), vj, preferred_element_type=F32)
            m_ref[c] = m_new

    n_full = (i * tq) // tk

    def full_chunk(j, carry):
        chunk(j, None)
        return carry

    lax.fori_loop(0, n_full, full_chunk, 0)
    rel = (lax.broadcasted_iota(jnp.int32, (tq, tk), 1) - lax.broadcasted_iota(jnp.int32, (tq, tk), 0))
    chunk(n_full, rel <= i * tq - n_full * tk)

    o1 = acc_ref[0] / l_ref[0]
    o2 = acc_ref[1] / l_ref[1]
    lam = (jnp.exp(jnp.sum(lq1_ref[...] * lk1_ref[...], axis=-1, keepdims=True))
           - jnp.exp(jnp.sum(lq2_ref[...] * lk2_ref[...], axis=-1, keepdims=True)) + lambda_init)
    o = o1 - lam * o2
    y = o * _rms_scale(o) * sw_ref[...]
    o_ref[...] = (y * (1.0 - lambda_init)).astype(o_ref.dtype)


def _diff_attention(qkv, lq1, lk1, lq2, lk2, subln_w, lambda_init, n_heads, dh):
    s = qkv.shape[0]
    hw = 2 * dh
    tk = _tile(s, 1024, V7X_LANES)
    tq = _tile(tk, 512, 16)
    slopes = 2.0 ** (-8.0 * jnp.arange(1, n_heads + 1, dtype=F32) / n_heads)
    col_bias = (slopes[:, None] * jnp.arange(tk, dtype=F32)[None, :]).reshape(n_heads, 1, tk)
    shift = jnp.broadcast_to((slopes * tk)[:, None, None], (n_heads, 1, V7X_LANES))
    kern = functools.partial(_diff_attn_kernel, tq=tq, tk=tk, dh=dh, scale=dh ** -0.5, lambda_init=lambda_init)
    vec = lambda a: a.reshape(1, -1).astype(F32)
    vmem = 2 * (2 * s * hw * 2 + 2 * tq * hw * 2) + 6 * tq * tk * 4 + 4 * tq * V7X_LANES * 4 * 2 + 4 * MIB
    return pl.pallas_call(
        kern,
        grid=(n_heads, s // tq),
        in_specs=[
            pl.BlockSpec((tq, hw), lambda h, i: (i, h)),
            pl.BlockSpec((s, hw), lambda h, i: (0, n_heads + h)),
            pl.BlockSpec((s, hw), lambda h, i: (0, 2 * n_heads + h)),
            pl.BlockSpec((1, 1, tk), lambda h, i: (h, 0, 0)),
            pl.BlockSpec((1, 1, V7X_LANES), lambda h, i: (h, 0, 0)),
            pl.BlockSpec((1, dh), lambda h, i: (0, 0)),
            pl.BlockSpec((1, dh), lambda h, i: (0, 0)),
            pl.BlockSpec((1, dh), lambda h, i: (0, 0)),
            pl.BlockSpec((1, dh), lambda h, i: (0, 0)),
            pl.BlockSpec((1, hw), lambda h, i: (0, 0)),
        ],
        out_specs=pl.BlockSpec((tq, hw), lambda h, i: (i, h)),
        out_shape=jax.ShapeDtypeStruct((s, n_heads * hw), BF16),
        scratch_shapes=[
            pltpu.VMEM((2, tq, 1), F32),
            pltpu.VMEM((2, tq, 1), F32),
            pltpu.VMEM((2, tq, hw), F32),
        ],
        compiler_params=_cparams(("parallel", "arbitrary"), vmem),
        name="diff_attention",
    )(qkv, qkv, qkv, col_bias, shift, vec(lq1), vec(lk1), vec(lq2), vec(lk2), vec(subln_w))


def _rg_lru_kernel(rx_ref, rg_ref, cw_ref, cb_ref, wa_ref, ba_ref, wx_ref, bx_ref, lam_ref, o_ref,
                   ext_ref, a_ref, b_ref, tail_ref, h_ref, *, t_rows, cw, blk):
    t = pl.program_id(1)
    halo = V7X_SUBLANES

    @pl.when(t == 0)
    def _():
        tail_ref[...] = jnp.zeros(tail_ref.shape, F32)
        h_ref[...] = jnp.zeros(h_ref.shape, F32)

    rx = rx_ref[...]
    ext_ref[0:halo, :] = tail_ref[...]
    ext_ref[halo:, :] = rx
    tail_ref[...] = rx[t_rows - halo:, :]
    taps = cw_ref[...]
    n_taps = taps.shape[0]
    xr = cb_ref[...]
    for j in range(n_taps):
        lag = n_taps - 1 - j
        xs = rx if lag == 0 else ext_ref[halo - lag:halo - lag + t_rows, :]
        xr = xr + xs * taps[j:j + 1, :]

    ra, ia = [], []
    for g in range(cw // blk):
        xb = xr[:, g * blk:(g + 1) * blk].astype(BF16)
        ra.append(jnp.dot(xb, wa_ref[g], preferred_element_type=F32))
        ia.append(jnp.dot(xb, wx_ref[g], preferred_element_type=F32))
    r = jax.nn.sigmoid(jnp.concatenate(ra, axis=-1) + ba_ref[...])
    ig = jax.nn.sigmoid(jnp.concatenate(ia, axis=-1) + bx_ref[...])
    neg_lam = -lam_ref[...]
    softplus = jnp.maximum(neg_lam, 0.0) + jnp.log1p(jnp.exp(-jnp.abs(neg_lam)))
    log_a = (-RG_C) * r * softplus
    th = jnp.tanh(log_a)
    a_ref[...] = jnp.exp(log_a)
    b_ref[...] = jnp.sqrt(-2.0 * th / (1.0 - th)) * (ig * xr)

    row = lax.broadcasted_iota(jnp.int32, (V7X_SUBLANES, cw), 0)

    def step(n, h):
        rows = pl.ds(pl.multiple_of(n * V7X_SUBLANES, V7X_SUBLANES), V7X_SUBLANES)
        a = a_ref[rows, :]
        b = b_ref[rows, :]
        d = 1
        while d < V7X_SUBLANES:
            keep = row >= d
            a_s = jnp.where(keep, pltpu.roll(a, d, 0), 1.0)
            b_s = jnp.where(keep, pltpu.roll(b, d, 0), 0.0)
            b = a * b_s + b
            a = a * a_s
            d *= 2
        hh = a * h + b
        b_ref[rows, :] = hh
        return hh[V7X_SUBLANES - 1:, :]

    h_last = lax.fori_loop(0, t_rows // V7X_SUBLANES, step, h_ref[0:1, :], unroll=4)
    h_ref[...] = jnp.broadcast_to(h_last, h_ref.shape)
    o_ref[...] = (b_ref[...] * jax.nn.gelu(rg_ref[...])).astype(o_ref.dtype)


def _rg_lru(rest, conv_w, conv_b, w_a, b_a, w_x, b_x, lru_lambda, d_rnn):
    s = rest.shape[0]
    n_blk, blk, _ = w_a.shape
    cw = _tile(d_rnn, 512, blk)
    t_rows = _tile(s, 1024, 16)
    n_cb = d_rnn // cw
    row = lambda a: a.reshape(1, -1).astype(F32)
    kern = functools.partial(_rg_lru_kernel, t_rows=t_rows, cw=cw, blk=blk)
    vec_spec = pl.BlockSpec((1, cw), lambda n, t: (0, n))
    gate_spec = pl.BlockSpec((cw // blk, blk, blk), lambda n, t: (n, 0, 0))
    vmem = 2 * (2 * t_rows * cw * 4 + t_rows * cw * 2) + 12 * t_rows * cw * 4 + 4 * MIB
    return pl.pallas_call(
        kern,
        grid=(n_cb, s // t_rows),
        in_specs=[
            pl.BlockSpec((t_rows, cw), lambda n, t: (t, n)),
            pl.BlockSpec((t_rows, cw), lambda n, t: (t, n_cb + n)),
            pl.BlockSpec((conv_w.shape[0], cw), lambda n, t: (0, n)),
            vec_spec, gate_spec, vec_spec, gate_spec, vec_spec, vec_spec,
        ],
        out_specs=pl.BlockSpec((t_rows, cw), lambda n, t: (t, n)),
        out_shape=jax.ShapeDtypeStruct((s, d_rnn), BF16),
        scratch_shapes=[
            pltpu.VMEM((t_rows + V7X_SUBLANES, cw), F32),
            pltpu.VMEM((t_rows, cw), F32),
            pltpu.VMEM((t_rows, cw), F32),
            pltpu.VMEM((V7X_SUBLANES, cw), F32),
            pltpu.VMEM((V7X_SUBLANES, cw), F32),
        ],
        compiler_params=_cparams(("parallel", "arbitrary"), vmem),
        name="rg_lru",
    )(rest, rest, conv_w.astype(F32), row(conv_b), w_a.astype(BF16), row(b_a), w_x.astype(BF16), row(b_x),
      row(lru_lambda))


def _gated_merge_kernel(ya_ref, yr_ref, wa_ref, wr_ref, ga_ref, gb_ref, o_ref):
    pa = jnp.dot(ya_ref[...], wa_ref[...], preferred_element_type=F32)
    pr = jnp.dot(yr_ref[...], wr_ref[...], preferred_element_type=F32)
    o_ref[...] = (jax.nn.sigmoid(ga_ref[...]) * pa + jax.nn.sigmoid(gb_ref[...]) * pr).astype(o_ref.dtype)


def _gated_merge(y_att, y_rec, w_pa, w_pr, rest, ga_col, gb_col):
    m, ka = y_att.shape
    _, kr = y_rec.shape
    n = w_pa.shape[1]
    tm = _tile(m, 1024, 16)
    tn = _tile(math.gcd(math.gcd(n, ga_col), gb_col), 512, V7X_LANES)
    vmem = 2 * (tm * (ka + kr) * 2 + (ka + kr) * tn * 2 + 2 * tm * tn * 4 + tm * tn * 2) + 4 * tm * tn * 4 + 2 * MIB
    return pl.pallas_call(
        _gated_merge_kernel,
        grid=(m // tm, n // tn),
        in_specs=[
            pl.BlockSpec((tm, ka), lambda i, j: (i, 0)),
            pl.BlockSpec((tm, kr), lambda i, j: (i, 0)),
            pl.BlockSpec((ka, tn), lambda i, j: (0, j)),
            pl.BlockSpec((kr, tn), lambda i, j: (0, j)),
            pl.BlockSpec((tm, tn), lambda i, j: (i, ga_col // tn + j)),
            pl.BlockSpec((tm, tn), lambda i, j: (i, gb_col // tn + j)),
        ],
        out_specs=pl.BlockSpec((tm, tn), lambda i, j: (i, j)),
        out_shape=jax.ShapeDtypeStruct((m, n), BF16),
        compiler_params=_cparams(("parallel", "arbitrary"), vmem),
        name="gated_merge",
    )(y_att, y_rec, w_pa, w_pr, rest, rest)


def _matmul_residual_kernel(a_ref, w_ref, x_ref, g_ref, o_ref):
    y = jnp.dot(a_ref[...], w_ref[...], preferred_element_type=F32)
    o_ref[...] = x_ref[...] + g_ref[...] * y


def _matmul_residual(a, w, x, mod, gate_idx):
    m, k = a.shape
    n = w.shape[1]
    tm = _tile(m, 1024, 16)
    tn = _tile(n, 512, V7X_LANES)
    vmem = 2 * (tm * k * 2 + k * tn * 2 + 2 * tm * tn * 4) + tm * tn * 4 + 2 * MIB
    return pl.pallas_call(
        _matmul_residual_kernel,
        grid=(m // tm, n // tn),
        in_specs=[
            pl.BlockSpec((tm, k), lambda i, j: (i, 0)),
            pl.BlockSpec((k, tn), lambda i, j: (0, j)),
            pl.BlockSpec((tm, tn), lambda i, j: (i, j)),
            pl.BlockSpec((1, tn), lambda i, j: (0, gate_idx * (n // tn) + j)),
        ],
        out_specs=pl.BlockSpec((tm, tn), lambda i, j: (i, j)),
        out_shape=jax.ShapeDtypeStruct((m, n), F32),
        compiler_params=_cparams(("parallel", "arbitrary"), vmem),
        name="matmul_residual",
    )(a, w, x, mod)


def _conv_ffn_kernel(h_ref, wg_ref, wu_ref, cw_ref, cb_ref, wd_ref, x_ref, g_ref, nw_ref, o_ref,
                     ext_ref, carry_ref, *, tm, dn, rn, final_norm):
    mi = pl.program_id(0)
    f = pl.program_id(1)
    nf = pl.num_programs(1)
    halo = V7X_SUBLANES

    @pl.when(mi == 0)
    def _():
        carry_ref[f] = jnp.zeros(carry_ref.shape[1:], F32)

    @pl.when(f == 0)
    def _():
        o_ref[...] = jnp.zeros(o_ref.shape, F32)

    h = h_ref[...]
    gate = jnp.dot(h, wg_ref[...], preferred_element_type=F32)
    up = jnp.dot(h, wu_ref[...], preferred_element_type=F32)
    ext_ref[0:halo, :] = carry_ref[f]
    ext_ref[halo:, :] = gate
    carry_ref[f] = gate[tm - halo:, :]
    taps = cw_ref[...]
    n_taps = taps.shape[0]
    gc = cb_ref[...]
    for j in range(n_taps):
        lag = n_taps - 1 - j
        gs = gate if lag == 0 else ext_ref[halo - lag:halo - lag + tm, :]
        gc = gc + gs * taps[j:j + 1, :]
    act = (jax.nn.gelu(gc) * up).astype(BF16)
    d = o_ref.shape[1]
    for c in range(d // dn):
        cols = slice(c * dn, (c + 1) * dn)
        o_ref[:, cols] += jnp.dot(act, wd_ref[:, cols], preferred_element_type=F32)

    @pl.when(f == nf - 1)
    def _():
        for r in range(tm // rn):
            rows = slice(r * rn, (r + 1) * rn)
            x2 = x_ref[rows, :] + g_ref[...] * o_ref[rows, :]
            if final_norm:
                x2 = x2 * _rms_scale(x2) * nw_ref[...]
            o_ref[rows, :] = x2


def _conv_ffn(h, w_up, conv_w, conv_b, w_down, x, mod, gate_idx, norm_w, final_norm):
    s, d = h.shape
    d_ff = w_down.shape[0]
    tm = _tile(s, 512, 16)
    tf = _tile(d_ff, 256, V7X_LANES)
    nf = d_ff // tf
    dn = _tile(d, 512, V7X_LANES)
    rn = _tile(tm, 128, V7X_SUBLANES)
    kern = functools.partial(_conv_ffn_kernel, tm=tm, dn=dn, rn=rn, final_norm=final_norm)
    vmem = (2 * (tm * d * 2 + 2 * d * tf * 2 + tf * d * 2 + tm * d * 4) + tm * d * 4
            + 8 * tm * tf * 4 + 4 * rn * d * 4 + tm * dn * 4 + 2 * MIB)
    return pl.pallas_call(
        kern,
        grid=(s // tm, nf),
        in_specs=[
            pl.BlockSpec((tm, d), lambda i, f: (i, 0)),
            pl.BlockSpec((d, tf), lambda i, f: (0, f)),
            pl.BlockSpec((d, tf), lambda i, f: (0, nf + f)),
            pl.BlockSpec((conv_w.shape[0], tf), lambda i, f: (0, f)),
            pl.BlockSpec((1, tf), lambda i, f: (0, f)),
            pl.BlockSpec((tf, d), lambda i, f: (f, 0)),
            pl.BlockSpec((tm, d), lambda i, f: (i, 0), pipeline_mode=pl.Buffered(1)),
            pl.BlockSpec((1, d), lambda i, f: (0, gate_idx)),
            pl.BlockSpec((1, d), lambda i, f: (0, 0)),
        ],
        out_specs=pl.BlockSpec((tm, d), lambda i, f: (i, 0)),
        out_shape=jax.ShapeDtypeStruct((s, d), F32),
        scratch_shapes=[
            pltpu.VMEM((tm + V7X_SUBLANES, tf), F32),
            pltpu.VMEM((nf, V7X_SUBLANES, tf), F32),
        ],
        compiler_params=_cparams(("arbitrary", "arbitrary"), vmem),
        name="conv_ffn",
    )(h, w_up, w_up, conv_w.astype(F32), conv_b.reshape(1, -1).astype(F32), w_down, x, mod,
      norm_w.reshape(1, -1).astype(F32))


def kernel(x, c, w_ada, b_ada, norm1_w, w_in, lambda_q1, lambda_k1, lambda_q2, lambda_k2, subln_w, conv_w, conv_b, w_rg_a, b_rg_a, w_rg_x, b_rg_x, lru_lambda, w_proj_attn, w_proj_rec, w_out, norm2_w, w_ffn_up, ffn_conv_w, ffn_conv_b, w_ffn_down, norm_f_w):
    batch, seq, d = x.shape
    depth = w_ada.shape[0]
    dh = lambda_q1.shape[-1]
    attn_w = w_proj_attn.shape[1]
    n_heads = attn_w // (2 * dh)
    d_rnn = conv_w.shape[-1]
    assert w_in.shape[-1] == 3 * attn_w + 2 * d_rnn + 2 * d
    row = lambda a: a.reshape(1, -1).astype(F32)

    outs = []
    for b in range(batch):
        xb = x[b]
        cb = c[b:b + 1]
        for l in range(depth):
            lambda_init = 0.8 - 0.6 * math.exp(-0.3 * l)
            mod = _adaln_mod(cb, w_ada[l], row(b_ada[l]))
            w_in_l = w_in[l].astype(BF16)
            h1 = _norm_modulate(xb, row(norm1_w[l]), mod, 0, 1)
            qkv = _matmul(h1, w_in_l[:, :3 * attn_w], BF16)
            rest = _matmul(h1, w_in_l[:, 3 * attn_w:], F32)
            y_att = _diff_attention(qkv, lambda_q1[l], lambda_k1[l], lambda_q2[l], lambda_k2[l], subln_w[l],
                                    lambda_init, n_heads, dh)
            y_rec = _rg_lru(rest, conv_w[l], conv_b[l], w_rg_a[l], b_rg_a[l], w_rg_x[l], b_rg_x[l],
                            lru_lambda[l], d_rnn)
            mixed = _gated_merge(y_att, y_rec, w_proj_attn[l].astype(BF16), w_proj_rec[l].astype(BF16), rest,
                                 2 * d_rnn, 2 * d_rnn + d)
            x1 = _matmul_residual(mixed, w_out[l].astype(BF16), xb, mod, 2)
            h2 = _norm_modulate(x1, row(norm2_w[l]), mod, 3, 4)
            last = l == depth - 1
            xb = _conv_ffn(h2, w_ffn_up[l].astype(BF16), ffn_conv_w[l], ffn_conv_b[l], w_ffn_down[l].astype(BF16),
                           x1, mod, 5, norm_f_w, final_norm=last)
        outs.append(xb)
    return jnp.stack(outs, axis=0)
```

```python
import functools
import math

import jax
import jax.numpy as jnp
from jax import lax
from jax.experimental import pallas as pl
from jax.experimental.pallas import tpu as pltpu

F32 = jnp.float32
BF16 = jnp.bfloat16
EPS = 1e-6
RG_C = 8.0
NEG_BIG = -1e30

V7X_LANES = 128
V7X_SUBLANES = 8
V7X_VMEM_LIMIT_BYTES = 60000 * 1024
MIB = 1024 * 1024


def _cparams(semantics, vmem_bytes):
    return pltpu.CompilerParams(
        dimension_semantics=semantics,
        vmem_limit_bytes=int(min(vmem_bytes, V7X_VMEM_LIMIT_BYTES)),
    )


def _tile(n, want, quantum):
    t = min(n, want) // quantum * quantum
    while t > quantum and n % t:
        t -= quantum
    assert t >= quantum and n % t == 0, (n, want, quantum)
    return t


def _rms_scale(x):
    return lax.rsqrt(jnp.mean(x * x, axis=-1, keepdims=True) + EPS)


def _adaln_mod_kernel(c_ref, w_ref, b_ref, o_ref):
    c = c_ref[...]
    a = c * jax.nn.sigmoid(c)
    a8 = jnp.broadcast_to(a, (V7X_SUBLANES, a.shape[1])).astype(BF16)
    acc = jnp.dot(a8, w_ref[...].astype(BF16), preferred_element_type=F32)
    o_ref[...] = acc[0:1, :] + b_ref[...]


def _adaln_mod(c, w, b):
    d, n = w.shape
    tn = _tile(n, 1024, V7X_LANES)
    return pl.pallas_call(
        _adaln_mod_kernel,
        grid=(n // tn,),
        in_specs=[
            pl.BlockSpec((1, d), lambda j: (0, 0)),
            pl.BlockSpec((d, tn), lambda j: (0, j)),
            pl.BlockSpec((1, tn), lambda j: (0, j)),
        ],
        out_specs=pl.BlockSpec((1, tn), lambda j: (0, j)),
        out_shape=jax.ShapeDtypeStruct((1, n), F32),
        compiler_params=_cparams(("arbitrary",), 2 * d * tn * 4 + d * tn * 2 + 4 * MIB),
        name="adaln_mod",
    )(c, w, b)


def _norm_modulate_kernel(x_ref, w_ref, scale_ref, shift_ref, o_ref):
    x = x_ref[...]
    y = x * _rms_scale(x) * w_ref[...]
    o_ref[...] = (y * (1.0 + scale_ref[...]) + shift_ref[...]).astype(o_ref.dtype)


def _norm_modulate(x, w, mod, shift_idx, scale_idx):
    s, d = x.shape
    tm = _tile(s, 256, 16)
    return pl.pallas_call(
        _norm_modulate_kernel,
        grid=(s // tm,),
        in_specs=[
            pl.BlockSpec((tm, d), lambda i: (i, 0)),
            pl.BlockSpec((1, d), lambda i: (0, 0)),
            pl.BlockSpec((1, d), lambda i: (0, scale_idx)),
            pl.BlockSpec((1, d), lambda i: (0, shift_idx)),
        ],
        out_specs=pl.BlockSpec((tm, d), lambda i: (i, 0)),
        out_shape=jax.ShapeDtypeStruct((s, d), BF16),
        compiler_params=_cparams(("parallel",), 2 * tm * d * 6 + 3 * tm * d * 4 + 2 * MIB),
        name="norm_modulate",
    )(x, w, mod, mod)


def _matmul_kernel(x_ref, w_ref, o_ref):
    o_ref[...] = jnp.dot(x_ref[...], w_ref[...], preferred_element_type=F32).astype(o_ref.dtype)


def _matmul_colscale_kernel(x_ref, w_ref, s_ref, o_ref):
    y = jnp.dot(x_ref[...], w_ref[...], preferred_element_type=F32)
    o_ref[...] = (y * s_ref[...]).astype(o_ref.dtype)


def _matmul(x, w, out_dtype, col_scale=None):
    m, k = x.shape
    _, n = w.shape
    tm = _tile(m, 1024, 16)
    tn = _tile(n, 512, V7X_LANES)
    ob = jnp.dtype(out_dtype).itemsize
    vmem = 2 * (tm * k * 2 + k * tn * 2 + tm * tn * ob) + 2 * tm * tn * 4 + 2 * MIB
    in_specs = [
        pl.BlockSpec((tm, k), lambda i, j: (i, 0)),
        pl.BlockSpec((k, tn), lambda i, j: (0, j)),
    ]
    args = (x, w)
    if col_scale is not None:
        in_specs.append(pl.BlockSpec((1, tn), lambda i, j: (0, j)))
        args = (x, w, col_scale)
    return pl.pallas_call(
        _matmul_kernel if col_scale is None else _matmul_colscale_kernel,
        grid=(m // tm, n // tn),
        in_specs=in_specs,
        out_specs=pl.BlockSpec((tm, tn), lambda i, j: (i, j)),
        out_shape=jax.ShapeDtypeStruct((m, n), out_dtype),
        compiler_params=_cparams(("parallel", "arbitrary"), vmem),
        name="matmul",
    )(*args)


LOG2E = math.log2(math.e)
POS_RADIX = 32
N_BIAS_COLS = 6


def _split3_bf16(x):
    hi = x.astype(BF16).astype(F32)
    mid = (x - hi).astype(BF16).astype(F32)
    lo = (x - hi - mid).astype(BF16).astype(F32)
    return hi, mid, lo


def _diff_attn_kernel(q_ref, k_ref, v_ref, pos_ref, ones_ref, qc_ref, sh_ref, lq1_ref, lk1_ref, lq2_ref, lk2_ref,
                      sw_ref, o_ref, ka_ref, va_ref, m_ref, acc_ref, z_ref, *, tq, tk, dh, lambda_init):
    i = pl.program_id(1)
    hw = 2 * dh
    s_len = k_ref.shape[0]

    @pl.when(i == 0)
    def _():
        lane_k = lax.broadcasted_iota(jnp.int32, (tk, hw), 1)

        def prep(j, carry):
            rows = pl.ds(pl.multiple_of(j * tk, tk), tk)
            kj = k_ref[rows, :]
            ka_ref[0, rows, :] = jnp.where(lane_k < dh, kj, pos_ref[0])
            ka_ref[1, rows, :] = jnp.where(lane_k >= dh, kj, pos_ref[1])
            va_ref[rows, 0:hw] = v_ref[rows, :]
            va_ref[rows, hw:2 * hw] = ones_ref[...]
            return carry

        lax.fori_loop(0, s_len // tk, prep, 0)

    q = q_ref[...]
    lane_q = lax.broadcasted_iota(jnp.int32, q.shape, 1)
    qc = qc_ref[0].astype(BF16)
    q_maps = (jnp.where(lane_q < dh, q, qc[0:1, :]), jnp.where(lane_q >= dh, q, qc[1:2, :]))
    frame_shift = sh_ref[0][:, 0:1]

    m_ref[...] = jnp.full(m_ref.shape, NEG_BIG, F32)
    acc_ref[...] = jnp.zeros(acc_ref.shape, F32)

    def scores(j, c):
        rows = pl.ds(pl.multiple_of(j * tk, tk), tk)
        return lax.dot_general(q_maps[c], ka_ref[c, rows, :], (((1,), (1,)), ((), ())),
                               preferred_element_type=F32)

    def lanes(x, width):
        return jnp.concatenate([x] * (width // V7X_LANES), axis=-1)

    def update(j, c, z):
        rows = pl.ds(pl.multiple_of(j * tk, tk), tk)
        m_prev = m_ref[c] - frame_shift
        m_new = jnp.maximum(m_prev, jnp.max(z, axis=-1, keepdims=True))
        alpha = jnp.exp2(m_prev - m_new)
        p = jnp.exp2(z - lanes(m_new, tk)).astype(BF16)
        acc_ref[c] = lanes(alpha, 2 * hw) * acc_ref[c] + jnp.dot(p, va_ref[rows, :], preferred_element_type=F32)
        m_ref[c] = m_new

    n_full = (i * tq) // tk

    def step(j, cur):
        z_ref[1 - cur, 0] = scores(j + 1, 0)
        z_ref[1 - cur, 1] = scores(j + 1, 1)
        update(j, 0, z_ref[cur, 0])
        update(j, 1, z_ref[cur, 1])

    def masked_step(cur):
        rel = (lax.broadcasted_iota(jnp.int32, (tq, tk), 1) - lax.broadcasted_iota(jnp.int32, (tq, tk), 0))
        mask = rel <= i * tq - n_full * tk
        update(n_full, 0, jnp.where(mask, z_ref[cur, 0], NEG_BIG))
        update(n_full, 1, jnp.where(mask, z_ref[cur, 1], NEG_BIG))

    z_ref[0, 0] = scores(0, 0)
    z_ref[0, 1] = scores(0, 1)

    def chunk_pair(t, carry):
        step(2 * t, 0)
        step(2 * t + 1, 1)
        return carry

    lax.fori_loop(0, n_full // 2, chunk_pair, 0)

    @pl.when(n_full % 2 == 1)
    def _():
        step(n_full - 1, 0)
        masked_step(1)

    @pl.when(n_full % 2 == 0)
    def _():
        masked_step(0)

    a0 = acc_ref[0]
    a1 = acc_ref[1]
    o1 = a0[:, 0:hw] / a0[:, hw:hw + 1]
    o2 = a1[:, 0:hw] / a1[:, hw:hw + 1]
    lam = (jnp.exp(jnp.sum(lq1_ref[...] * lk1_ref[...], axis=-1, keepdims=True))
           - jnp.exp(jnp.sum(lq2_ref[...] * lk2_ref[...], axis=-1, keepdims=True)) + lambda_init)
    o = o1 - lam * o2
    y = o * _rms_scale(o) * sw_ref[...]
    o_ref[...] = (y * (1.0 - lambda_init)).astype(o_ref.dtype)


def _diff_attention(qkv, lq1, lk1, lq2, lk2, subln_w, lambda_init, n_heads, dh):
    s = qkv.shape[0]
    hw = 2 * dh
    tk = _tile(s, 512, V7X_LANES)
    tq = _tile(tk, 512, 16)
    nb = N_BIAS_COLS
    assert dh >= nb and tk // POS_RADIX <= 256
    slopes2 = (2.0 ** (-8.0 * jnp.arange(1, n_heads + 1, dtype=F32) / n_heads)) * LOG2E
    pieces = jnp.stack(_split3_bf16(slopes2), axis=-1)
    q_cols = jnp.concatenate([pieces * POS_RADIX, pieces], axis=-1)
    qc = jnp.zeros((n_heads, 2, hw), F32).at[:, 0, dh:dh + nb].set(q_cols).at[:, 1, 0:nb].set(q_cols)
    jj = jnp.arange(tk, dtype=jnp.int32)
    k_cols = jnp.stack([jj // POS_RADIX] * 3 + [jj % POS_RADIX] * 3, axis=-1).astype(F32)
    pos = jnp.zeros((2, tk, hw), F32).at[0, :, dh:dh + nb].set(k_cols).at[1, :, 0:nb].set(k_cols).astype(BF16)
    ones = jnp.zeros((tk, hw), F32).at[:, 0].set(1.0).astype(BF16)
    shift = jnp.broadcast_to((slopes2 * tk)[:, None, None], (n_heads, 1, V7X_LANES))
    kern = functools.partial(_diff_attn_kernel, tq=tq, tk=tk, dh=dh, lambda_init=lambda_init)
    vec = lambda a: a.reshape(1, -1).astype(F32)
    const2 = lambda h, i: (0, 0)
    vmem = (2 * (2 * s * hw * 2 + 2 * tq * hw * 2 + 3 * tk * hw * 2) + 4 * s * hw * 2
            + 8 * tq * tk * 4 + 2 * tq * (2 * hw + V7X_LANES) * 4 + 4 * MIB)
    return pl.pallas_call(
        kern,
        grid=(n_heads, s // tq),
        in_specs=[
            pl.BlockSpec((tq, hw), lambda h, i: (i, h)),
            pl.BlockSpec((s, hw), lambda h, i: (0, n_heads + h)),
            pl.BlockSpec((s, hw), lambda h, i: (0, 2 * n_heads + h)),
            pl.BlockSpec((2, tk, hw), lambda h, i: (0, 0, 0)),
            pl.BlockSpec((tk, hw), const2),
            pl.BlockSpec((1, 2, hw), lambda h, i: (h, 0, 0)),
            pl.BlockSpec((1, 1, V7X_LANES), lambda h, i: (h, 0, 0)),
            pl.BlockSpec((1, dh), const2),
            pl.BlockSpec((1, dh), const2),
            pl.BlockSpec((1, dh), const2),
            pl.BlockSpec((1, dh), const2),
            pl.BlockSpec((1, hw), const2),
        ],
        out_specs=pl.BlockSpec((tq, hw), lambda h, i: (i, h)),
        out_shape=jax.ShapeDtypeStruct((s, n_heads * hw), BF16),
        scratch_shapes=[
            pltpu.VMEM((2, s, hw), BF16),
            pltpu.VMEM((s, 2 * hw), BF16),
            pltpu.VMEM((2, tq, V7X_LANES), F32),
            pltpu.VMEM((2, tq, 2 * hw), F32),
            pltpu.VMEM((2, 2, tq, tk), F32),
        ],
        compiler_params=_cparams(("arbitrary", "arbitrary"), vmem),
        name="diff_attention",
    )(qkv, qkv, qkv, pos, ones, qc, shift, vec(lq1), vec(lk1), vec(lq2), vec(lk2), vec(subln_w))


def _rg_lru_kernel(rx_ref, rg_ref, cw_ref, cb_ref, wa_ref, ba_ref, wx_ref, bx_ref, lam_ref, o_ref,
                   ext_ref, a_ref, b_ref, tail_ref, h_ref, *, t_rows, cw, blk):
    t = pl.program_id(1)
    halo = V7X_SUBLANES

    @pl.when(t == 0)
    def _():
        tail_ref[...] = jnp.zeros(tail_ref.shape, F32)
        h_ref[...] = jnp.zeros(h_ref.shape, F32)

    rx = rx_ref[...]
    ext_ref[0:halo, :] = tail_ref[...]
    ext_ref[halo:, :] = rx
    tail_ref[...] = rx[t_rows - halo:, :]
    taps = cw_ref[...]
    n_taps = taps.shape[0]
    xr = cb_ref[...]
    for j in range(n_taps):
        lag = n_taps - 1 - j
        xs = rx if lag == 0 else ext_ref[halo - lag:halo - lag + t_rows, :]
        xr = xr + xs * taps[j:j + 1, :]

    ra, ia = [], []
    for g in range(cw // blk):
        xb = xr[:, g * blk:(g + 1) * blk].astype(BF16)
        ra.append(jnp.dot(xb, wa_ref[g], preferred_element_type=F32))
        ia.append(jnp.dot(xb, wx_ref[g], preferred_element_type=F32))
    r = jax.nn.sigmoid(jnp.concatenate(ra, axis=-1) + ba_ref[...])
    ig = jax.nn.sigmoid(jnp.concatenate(ia, axis=-1) + bx_ref[...])
    neg_lam = -lam_ref[...]
    softplus = jnp.maximum(neg_lam, 0.0) + jnp.log1p(jnp.exp(-jnp.abs(neg_lam)))
    log_a = (-RG_C) * r * softplus
    th = jnp.tanh(log_a)
    a_ref[...] = jnp.exp(log_a)
    b_ref[...] = jnp.sqrt(-2.0 * th / (1.0 - th)) * (ig * xr)

    row = lax.broadcasted_iota(jnp.int32, (V7X_SUBLANES, cw), 0)

    def step(n, h):
        rows = pl.ds(pl.multiple_of(n * V7X_SUBLANES, V7X_SUBLANES), V7X_SUBLANES)
        a = a_ref[rows, :]
        b = b_ref[rows, :]
        d = 1
        while d < V7X_SUBLANES:
            keep = row >= d
            a_s = jnp.where(keep, pltpu.roll(a, d, 0), 1.0)
            b_s = jnp.where(keep, pltpu.roll(b, d, 0), 0.0)
            b = a * b_s + b
            a = a * a_s
            d *= 2
        hh = a * h + b
        b_ref[rows, :] = hh
        return hh[V7X_SUBLANES - 1:, :]

    h_last = lax.fori_loop(0, t_rows // V7X_SUBLANES, step, h_ref[0:1, :], unroll=4)
    h_ref[...] = jnp.broadcast_to(h_last, h_ref.shape)
    o_ref[...] = (b_ref[...] * jax.nn.gelu(rg_ref[...])).astype(o_ref.dtype)


def _rg_lru(rest, conv_w, conv_b, w_a, b_a, w_x, b_x, lru_lambda, d_rnn):
    s = rest.shape[0]
    n_blk, blk, _ = w_a.shape
    cw = _tile(d_rnn, 512, blk)
    t_rows = _tile(s, 1024, 16)
    n_cb = d_rnn // cw
    row = lambda a: a.reshape(1, -1).astype(F32)
    kern = functools.partial(_rg_lru_kernel, t_rows=t_rows, cw=cw, blk=blk)
    vec_spec = pl.BlockSpec((1, cw), lambda n, t: (0, n))
    gate_spec = pl.BlockSpec((cw // blk, blk, blk), lambda n, t: (n, 0, 0))
    vmem = 2 * (2 * t_rows * cw * 4 + t_rows * cw * 2) + 12 * t_rows * cw * 4 + 4 * MIB
    return pl.pallas_call(
        kern,
        grid=(n_cb, s // t_rows),
        in_specs=[
            pl.BlockSpec((t_rows, cw), lambda n, t: (t, n)),
            pl.BlockSpec((t_rows, cw), lambda n, t: (t, n_cb + n)),
            pl.BlockSpec((conv_w.shape[0], cw), lambda n, t: (0, n)),
            vec_spec, gate_spec, vec_spec, gate_spec, vec_spec, vec_spec,
        ],
        out_specs=pl.BlockSpec((t_rows, cw), lambda n, t: (t, n)),
        out_shape=jax.ShapeDtypeStruct((s, d_rnn), BF16),
        scratch_shapes=[
            pltpu.VMEM((t_rows + V7X_SUBLANES, cw), F32),
            pltpu.VMEM((t_rows, cw), F32),
            pltpu.VMEM((t_rows, cw), F32),
            pltpu.VMEM((V7X_SUBLANES, cw), F32),
            pltpu.VMEM((V7X_SUBLANES, cw), F32),
        ],
        compiler_params=_cparams(("parallel", "arbitrary"), vmem),
        name="rg_lru",
    )(rest, rest, conv_w.astype(F32), row(conv_b), w_a.astype(BF16), row(b_a), w_x.astype(BF16), row(b_x),
      row(lru_lambda))


def _gated_merge_kernel(ya_ref, yr_ref, wa_ref, wr_ref, ga_ref, gb_ref, o_ref):
    pa = jnp.dot(ya_ref[...], wa_ref[...], preferred_element_type=F32)
    pr = jnp.dot(yr_ref[...], wr_ref[...], preferred_element_type=F32)
    o_ref[...] = (jax.nn.sigmoid(ga_ref[...]) * pa + jax.nn.sigmoid(gb_ref[...]) * pr).astype(o_ref.dtype)


def _gated_merge(y_att, y_rec, w_pa, w_pr, rest, ga_col, gb_col):
    m, ka = y_att.shape
    _, kr = y_rec.shape
    n = w_pa.shape[1]
    tm = _tile(m, 1024, 16)
    tn = _tile(math.gcd(math.gcd(n, ga_col), gb_col), 512, V7X_LANES)
    vmem = 2 * (tm * (ka + kr) * 2 + (ka + kr) * tn * 2 + 2 * tm * tn * 4 + tm * tn * 2) + 4 * tm * tn * 4 + 2 * MIB
    return pl.pallas_call(
        _gated_merge_kernel,
        grid=(m // tm, n // tn),
        in_specs=[
            pl.BlockSpec((tm, ka), lambda i, j: (i, 0)),
            pl.BlockSpec((tm, kr), lambda i, j: (i, 0)),
            pl.BlockSpec((ka, tn), lambda i, j: (0, j)),
            pl.BlockSpec((kr, tn), lambda i, j: (0, j)),
            pl.BlockSpec((tm, tn), lambda i, j: (i, ga_col // tn + j)),
            pl.BlockSpec((tm, tn), lambda i, j: (i, gb_col // tn + j)),
        ],
        out_specs=pl.BlockSpec((tm, tn), lambda i, j: (i, j)),
        out_shape=jax.ShapeDtypeStruct((m, n), BF16),
        compiler_params=_cparams(("parallel", "arbitrary"), vmem),
        name="gated_merge",
    )(y_att, y_rec, w_pa, w_pr, rest, rest)


def _matmul_residual_kernel(a_ref, w_ref, x_ref, g_ref, o_ref):
    y = jnp.dot(a_ref[...], w_ref[...], preferred_element_type=F32)
    o_ref[...] = x_ref[...] + g_ref[...] * y


def _matmul_residual(a, w, x, mod, gate_idx):
    m, k = a.shape
    n = w.shape[1]
    tm = _tile(m, 1024, 16)
    tn = _tile(n, 512, V7X_LANES)
    vmem = 2 * (tm * k * 2 + k * tn * 2 + 2 * tm * tn * 4) + tm * tn * 4 + 2 * MIB
    return pl.pallas_call(
        _matmul_residual_kernel,
        grid=(m // tm, n // tn),
        in_specs=[
            pl.BlockSpec((tm, k), lambda i, j: (i, 0)),
            pl.BlockSpec((k, tn), lambda i, j: (0, j)),
            pl.BlockSpec((tm, tn), lambda i, j: (i, j)),
            pl.BlockSpec((1, tn), lambda i, j: (0, gate_idx * (n // tn) + j)),
        ],
        out_specs=pl.BlockSpec((tm, tn), lambda i, j: (i, j)),
        out_shape=jax.ShapeDtypeStruct((m, n), F32),
        compiler_params=_cparams(("parallel", "arbitrary"), vmem),
        name="matmul_residual",
    )(a, w, x, mod)


def _conv_ffn_kernel(h_ref, wg_ref, wu_ref, cw_ref, cb_ref, wd_ref, x_ref, g_ref, nw_ref, o_ref,
                     gate_ref, up_ref, act_ref, carry_ref, *, tm, dn, rn, rc, nf, final_norm):
    mi = pl.program_id(0)
    f = pl.program_id(1)
    halo = V7X_SUBLANES
    d = o_ref.shape[1]

    def gate_up(slot):
        @pl.when(mi == 0)
        def _():
            carry_ref[f] = jnp.zeros(carry_ref.shape[1:], F32)

        h = h_ref[...]
        gate = jnp.dot(h, wg_ref[...], preferred_element_type=F32)
        up_ref[slot] = jnp.dot(h, wu_ref[...], preferred_element_type=F32)
        gate_ref[slot, 0:halo, :] = carry_ref[f]
        gate_ref[slot, halo:, :] = gate
        carry_ref[f] = gate[tm - halo:, :]

    def activation_rows(slot, r):
        r0 = r * rc
        taps = cw_ref[...]
        n_taps = taps.shape[0]
        gc = cb_ref[...]
        for j in range(n_taps):
            start = halo - (n_taps - 1 - j) + r0
            gc = gc + gate_ref[slot, start:start + rc, :] * taps[j:j + 1, :]
        act_ref[slot, r0:r0 + rc, :] = (jax.nn.gelu(gc) * up_ref[slot, r0:r0 + rc, :]).astype(BF16)

    def down_cols(slot, c):
        cols = slice(c * dn, (c + 1) * dn)
        o_ref[:, cols] += jnp.dot(act_ref[slot], wd_ref[:, cols], preferred_element_type=F32)

    def step(gu_slot, act_slot, down_slot):
        n_c = d // dn
        n_r = tm // rc
        if gu_slot is not None:
            gate_up(gu_slot)
        for k in range(max(n_c, n_r)):
            if act_slot is not None and k < n_r:
                activation_rows(act_slot, k)
            if down_slot is not None and k < n_c:
                down_cols(down_slot, k)

    @pl.when(f == 0)
    def _():
        o_ref[...] = jnp.zeros(o_ref.shape, F32)
        step(0, None, None)

    @pl.when(f == 1)
    def _():
        step(1, 0, None)

    @pl.when(jnp.logical_and(jnp.logical_and(f >= 2, f < nf), f % 2 == 0))
    def _():
        step(0, 1, 0)

    @pl.when(jnp.logical_and(jnp.logical_and(f >= 2, f < nf), f % 2 == 1))
    def _():
        step(1, 0, 1)

    @pl.when(f == nf)
    def _():
        step(None, (nf - 1) % 2, nf % 2)

    @pl.when(f == nf + 1)
    def _():
        step(None, None, (nf - 1) % 2)
        for r in range(tm // rn):
            rows = slice(r * rn, (r + 1) * rn)
            x2 = x_ref[rows, :] + g_ref[...] * o_ref[rows, :]
            if final_norm:
                x2 = x2 * _rms_scale(x2) * nw_ref[...]
            o_ref[rows, :] = x2


def _conv_ffn(h, w_up, conv_w, conv_b, w_down, x, mod, gate_idx, norm_w, final_norm):
    s, d = h.shape
    d_ff = w_down.shape[0]
    tm = _tile(s, 512, 16)
    tf = _tile(d_ff, 256, V7X_LANES)
    nf = d_ff // tf
    assert nf >= 2
    dn = _tile(d, 512, V7X_LANES)
    rn = _tile(tm, 128, V7X_SUBLANES)
    rc = _tile(tm, 64, 16)
    kern = functools.partial(_conv_ffn_kernel, tm=tm, dn=dn, rn=rn, rc=rc, nf=nf, final_norm=final_norm)
    vmem = (2 * (tm * d * 2 + 2 * d * tf * 2 + tf * d * 2 + tm * d * 4) + tm * d * 4
            + 12 * tm * tf * 4 + 4 * rn * d * 4 + tm * dn * 4 + 2 * MIB)
    chunk = lambda f, lag: jnp.clip(f - lag, 0, nf - 1)
    return pl.pallas_call(
        kern,
        grid=(s // tm, nf + 2),
        in_specs=[
            pl.BlockSpec((tm, d), lambda i, f: (i, 0)),
            pl.BlockSpec((d, tf), lambda i, f: (0, chunk(f, 0))),
            pl.BlockSpec((d, tf), lambda i, f: (0, nf + chunk(f, 0))),
            pl.BlockSpec((conv_w.shape[0], tf), lambda i, f: (0, chunk(f, 1))),
            pl.BlockSpec((1, tf), lambda i, f: (0, chunk(f, 1))),
            pl.BlockSpec((tf, d), lambda i, f: (chunk(f, 2), 0)),
            pl.BlockSpec((tm, d), lambda i, f: (i, 0), pipeline_mode=pl.Buffered(1)),
            pl.BlockSpec((1, d), lambda i, f: (0, gate_idx)),
            pl.BlockSpec((1, d), lambda i, f: (0, 0)),
        ],
        out_specs=pl.BlockSpec((tm, d), lambda i, f: (i, 0)),
        out_shape=jax.ShapeDtypeStruct((s, d), F32),
        scratch_shapes=[
            pltpu.VMEM((2, tm + V7X_SUBLANES, tf), F32),
            pltpu.VMEM((2, tm, tf), F32),
            pltpu.VMEM((2, tm, tf), BF16),
            pltpu.VMEM((nf, V7X_SUBLANES, tf), F32),
        ],
        compiler_params=_cparams(("arbitrary", "arbitrary"), vmem),
        name="conv_ffn",
    )(h, w_up, w_up, conv_w.astype(F32), conv_b.reshape(1, -1).astype(F32), w_down, x, mod,
      norm_w.reshape(1, -1).astype(F32))


def kernel(x, c, w_ada, b_ada, norm1_w, w_in, lambda_q1, lambda_k1, lambda_q2, lambda_k2, subln_w, conv_w, conv_b, w_rg_a, b_rg_a, w_rg_x, b_rg_x, lru_lambda, w_proj_attn, w_proj_rec, w_out, norm2_w, w_ffn_up, ffn_conv_w, ffn_conv_b, w_ffn_down, norm_f_w):
    batch, seq, d = x.shape
    depth = w_ada.shape[0]
    dh = lambda_q1.shape[-1]
    attn_w = w_proj_attn.shape[1]
    n_heads = attn_w // (2 * dh)
    d_rnn = conv_w.shape[-1]
    assert w_in.shape[-1] == 3 * attn_w + 2 * d_rnn + 2 * d
    row = lambda a: a.reshape(1, -1).astype(F32)

    outs = []
    for b in range(batch):
        xb = x[b]
        cb = c[b:b + 1]
        for l in range(depth):
            lambda_init = 0.8 - 0.6 * math.exp(-0.3 * l)
            mod = _adaln_mod(cb, w_ada[l], row(b_ada[l]))
            w_in_l = w_in[l].astype(BF16)
            h1 = _norm_modulate(xb, row(norm1_w[l]), mod, 0, 1)
            q_scale = jnp.concatenate([jnp.full((1, attn_w), dh ** -0.5 * LOG2E, F32),
                                       jnp.ones((1, 2 * attn_w), F32)], axis=1)
            qkv = _matmul(h1, w_in_l[:, :3 * attn_w], BF16, q_scale)
            rest = _matmul(h1, w_in_l[:, 3 * attn_w:], F32)
            y_att = _diff_attention(qkv, lambda_q1[l], lambda_k1[l], lambda_q2[l], lambda_k2[l], subln_w[l],
                                    lambda_init, n_heads, dh)
            y_rec = _rg_lru(rest, conv_w[l], conv_b[l], w_rg_a[l], b_rg_a[l], w_rg_x[l], b_rg_x[l],
                            lru_lambda[l], d_rnn)
            mixed = _gated_merge(y_att, y_rec, w_proj_attn[l].astype(BF16), w_proj_rec[l].astype(BF16), rest,
                                 2 * d_rnn, 2 * d_rnn + d)
            x1 = _matmul_residual(mixed, w_out[l].astype(BF16), xb, mod, 2)
            h2 = _norm_modulate(x1, row(norm2_w[l]), mod, 3, 4)
            last = l == depth - 1
            xb = _conv_ffn(h2, w_ffn_up[l].astype(BF16), ffn_conv_w[l], ffn_conv_b[l], w_ffn_down[l].astype(BF16),
                           x1, mod, 5, norm_f_w, final_norm=last)
        outs.append(xb)
    return outs[0][None] if batch == 1 else jnp.stack(outs, axis=0)
```

```python
import functools
import math

import jax
import jax.numpy as jnp
from jax import lax
from jax.experimental import pallas as pl
from jax.experimental.pallas import tpu as pltpu

F32 = jnp.float32
BF16 = jnp.bfloat16
EPS = 1e-6
RG_C = 8.0
NEG_BIG = -1e30

V7X_LANES = 128
V7X_SUBLANES = 8
V7X_VMEM_LIMIT_BYTES = 60000 * 1024
MIB = 1024 * 1024


def _cparams(semantics, vmem_bytes):
    return pltpu.CompilerParams(
        dimension_semantics=semantics,
        vmem_limit_bytes=int(min(vmem_bytes, V7X_VMEM_LIMIT_BYTES)),
    )


def _tile(n, want, quantum):
    t = min(n, want) // quantum * quantum
    while t > quantum and n % t:
        t -= quantum
    assert t >= quantum and n % t == 0, (n, want, quantum)
    return t


def _rms_scale(x):
    return lax.rsqrt(jnp.mean(x * x, axis=-1, keepdims=True) + EPS)


def _adaln_mod_kernel(c_ref, w_ref, b_ref, o_ref):
    c = c_ref[...]
    a = c * jax.nn.sigmoid(c)
    a8 = jnp.broadcast_to(a, (V7X_SUBLANES, a.shape[1])).astype(BF16)
    acc = jnp.dot(a8, w_ref[...].astype(BF16), preferred_element_type=F32)
    o_ref[...] = acc[0:1, :] + b_ref[...]


def _adaln_mod(c, w, b):
    d, n = w.shape
    tn = _tile(n, 1024, V7X_LANES)
    return pl.pallas_call(
        _adaln_mod_kernel,
        grid=(n // tn,),
        in_specs=[
            pl.BlockSpec((1, d), lambda j: (0, 0)),
            pl.BlockSpec((d, tn), lambda j: (0, j)),
            pl.BlockSpec((1, tn), lambda j: (0, j)),
        ],
        out_specs=pl.BlockSpec((1, tn), lambda j: (0, j)),
        out_shape=jax.ShapeDtypeStruct((1, n), F32),
        compiler_params=_cparams(("arbitrary",), 2 * d * tn * 4 + d * tn * 2 + 4 * MIB),
        name="adaln_mod",
    )(c, w, b)


def _norm_modulate_kernel(x_ref, w_ref, scale_ref, shift_ref, o_ref):
    x = x_ref[...]
    y = x * _rms_scale(x) * w_ref[...]
    o_ref[...] = (y * (1.0 + scale_ref[...]) + shift_ref[...]).astype(o_ref.dtype)


def _norm_modulate(x, w, mod, shift_idx, scale_idx):
    s, d = x.shape
    tm = _tile(s, 256, 16)
    return pl.pallas_call(
        _norm_modulate_kernel,
        grid=(s // tm,),
        in_specs=[
            pl.BlockSpec((tm, d), lambda i: (i, 0)),
            pl.BlockSpec((1, d), lambda i: (0, 0)),
            pl.BlockSpec((1, d), lambda i: (0, scale_idx)),
            pl.BlockSpec((1, d), lambda i: (0, shift_idx)),
        ],
        out_specs=pl.BlockSpec((tm, d), lambda i: (i, 0)),
        out_shape=jax.ShapeDtypeStruct((s, d), BF16),
        compiler_params=_cparams(("parallel",), 2 * tm * d * 6 + 3 * tm * d * 4 + 2 * MIB),
        name="norm_modulate",
    )(x, w, mod, mod)


def _cast_weight_once(w_ref, wb_ref):
    @pl.when(pl.program_id(1) == 0)
    def _():
        wb_ref[...] = w_ref[...].astype(BF16)


def _matmul_kernel(x_ref, w_ref, o_ref, wb_ref):
    _cast_weight_once(w_ref, wb_ref)
    o_ref[...] = jnp.dot(x_ref[...], wb_ref[...], preferred_element_type=F32).astype(o_ref.dtype)


def _matmul_colscale_kernel(x_ref, w_ref, s_ref, o_ref, wb_ref):
    _cast_weight_once(w_ref, wb_ref)
    y = jnp.dot(x_ref[...], wb_ref[...], preferred_element_type=F32)
    o_ref[...] = (y * s_ref[...]).astype(o_ref.dtype)


def _matmul(x, w, col0, n, out_dtype, col_scale=None):
    m, k = x.shape
    tm = _tile(m, 1024, 16)
    tn = _tile(math.gcd(n, col0) if col0 else n, 512, V7X_LANES)
    ob = jnp.dtype(out_dtype).itemsize
    vmem = 2 * (tm * k * 2 + k * tn * 4 + tm * tn * ob) + k * tn * 2 + 2 * tm * tn * 4 + 2 * MIB
    in_specs = [
        pl.BlockSpec((tm, k), lambda j, i: (i, 0)),
        pl.BlockSpec((k, tn), lambda j, i: (0, col0 // tn + j)),
    ]
    args = (x, w)
    if col_scale is not None:
        in_specs.append(pl.BlockSpec((1, tn), lambda j, i: (0, j)))
        args = (x, w, col_scale)
    return pl.pallas_call(
        _matmul_kernel if col_scale is None else _matmul_colscale_kernel,
        grid=(n // tn, m // tm),
        in_specs=in_specs,
        out_specs=pl.BlockSpec((tm, tn), lambda j, i: (i, j)),
        out_shape=jax.ShapeDtypeStruct((m, n), out_dtype),
        scratch_shapes=[pltpu.VMEM((k, tn), BF16)],
        compiler_params=_cparams(("arbitrary", "arbitrary"), vmem),
        name="matmul",
    )(*args)


LOG2E = math.log2(math.e)
POS_RADIX = 32
N_BIAS_COLS = 6


def _split3_bf16(x):
    hi = x.astype(BF16).astype(F32)
    mid = (x - hi).astype(BF16).astype(F32)
    lo = (x - hi - mid).astype(BF16).astype(F32)
    return hi, mid, lo


def _diff_attn_kernel(q_ref, k_ref, v_ref, pos_ref, ones_ref, mb_ref, qc_ref, sh_ref, lq1_ref, lk1_ref, lq2_ref, lk2_ref,
                      sw_ref, o_ref, ka_ref, va_ref, m_ref, acc_ref, z_ref, *, tq, tk, dh, lambda_init):
    i = pl.program_id(1)
    hw = 2 * dh
    s_len = k_ref.shape[0]

    @pl.when(i == 0)
    def _():
        lane_k = lax.broadcasted_iota(jnp.int32, (tk, hw), 1)

        def prep(j, carry):
            rows = pl.ds(pl.multiple_of(j * tk, tk), tk)
            kj = k_ref[rows, :]
            ka_ref[0, rows, :] = jnp.where(lane_k < dh, kj, pos_ref[0])
            ka_ref[1, rows, :] = jnp.where(lane_k >= dh, kj, pos_ref[1])
            va_ref[rows, 0:hw] = v_ref[rows, :]
            va_ref[rows, hw:2 * hw] = ones_ref[...]
            return carry

        lax.fori_loop(0, s_len // tk, prep, 0)

    q = q_ref[...]
    lane_q = lax.broadcasted_iota(jnp.int32, q.shape, 1)
    qc = qc_ref[0].astype(BF16)
    q_maps = (jnp.where(lane_q < dh, q, qc[0:1, :]), jnp.where(lane_q >= dh, q, qc[1:2, :]))
    frame_shift = sh_ref[0][:, 0:1]

    m_ref[...] = jnp.full(m_ref.shape, NEG_BIG, F32)
    acc_ref[...] = jnp.zeros(acc_ref.shape, F32)

    def scores(j, c):
        rows = pl.ds(pl.multiple_of(j * tk, tk), tk)
        return lax.dot_general(q_maps[c], ka_ref[c, rows, :], (((1,), (1,)), ((), ())),
                               preferred_element_type=F32)

    def lanes(x, width):
        return jnp.concatenate([x] * (width // V7X_LANES), axis=-1)

    def update(j, c, z):
        rows = pl.ds(pl.multiple_of(j * tk, tk), tk)
        m_prev = m_ref[c] - frame_shift
        m_new = jnp.maximum(m_prev, jnp.max(z, axis=-1, keepdims=True))
        alpha = jnp.exp2(m_prev - m_new)
        p = jnp.exp2(z - lanes(m_new, tk)).astype(BF16)
        acc_ref[c] = lanes(alpha, 2 * hw) * acc_ref[c] + jnp.dot(p, va_ref[rows, :], preferred_element_type=F32)
        m_ref[c] = m_new

    n_full = (i * tq) // tk

    def step(j, cur):
        z_ref[1 - cur, 0] = scores(j + 1, 0)
        z_ref[1 - cur, 1] = scores(j + 1, 1)
        update(j, 0, z_ref[cur, 0])
        update(j, 1, z_ref[cur, 1])

    def masked_step(cur):
        causal = mb_ref[0]
        update(n_full, 0, z_ref[cur, 0] + causal)
        update(n_full, 1, z_ref[cur, 1] + causal)

    z_ref[0, 0] = scores(0, 0)
    z_ref[0, 1] = scores(0, 1)

    def chunk_pair(t, carry):
        step(2 * t, 0)
        step(2 * t + 1, 1)
        return carry

    lax.fori_loop(0, n_full // 2, chunk_pair, 0)

    @pl.when(n_full % 2 == 1)
    def _():
        step(n_full - 1, 0)
        masked_step(1)

    @pl.when(n_full % 2 == 0)
    def _():
        masked_step(0)

    a0 = acc_ref[0]
    a1 = acc_ref[1]
    o1 = a0[:, 0:hw] / a0[:, hw:hw + 1]
    o2 = a1[:, 0:hw] / a1[:, hw:hw + 1]
    lam = (jnp.exp(jnp.sum(lq1_ref[...] * lk1_ref[...], axis=-1, keepdims=True))
           - jnp.exp(jnp.sum(lq2_ref[...] * lk2_ref[...], axis=-1, keepdims=True)) + lambda_init)
    o = o1 - lam * o2
    y = o * _rms_scale(o) * sw_ref[...]
    o_ref[...] = (y * (1.0 - lambda_init)).astype(o_ref.dtype)


def _diff_attention(qkv, lq1, lk1, lq2, lk2, subln_w, lambda_init, n_heads, dh):
    s = qkv.shape[0]
    hw = 2 * dh
    tk = _tile(s, 1024, V7X_LANES)
    tq = _tile(tk, 512, 16)
    nb = N_BIAS_COLS
    assert dh >= nb and tk // POS_RADIX <= 256
    slopes2 = (2.0 ** (-8.0 * jnp.arange(1, n_heads + 1, dtype=F32) / n_heads)) * LOG2E
    pieces = jnp.stack(_split3_bf16(slopes2), axis=-1)
    q_cols = jnp.concatenate([pieces * POS_RADIX, pieces], axis=-1)
    qc = jnp.zeros((n_heads, 2, hw), F32).at[:, 0, dh:dh + nb].set(q_cols).at[:, 1, 0:nb].set(q_cols)
    jj = jnp.arange(tk, dtype=jnp.int32)
    k_cols = jnp.stack([jj // POS_RADIX] * 3 + [jj % POS_RADIX] * 3, axis=-1).astype(F32)
    pos = jnp.zeros((2, tk, hw), F32).at[0, :, dh:dh + nb].set(k_cols).at[1, :, 0:nb].set(k_cols).astype(BF16)
    ones = jnp.zeros((tk, hw), F32).at[:, 0].set(1.0).astype(BF16)
    shift = jnp.broadcast_to((slopes2 * tk)[:, None, None], (n_heads, 1, V7X_LANES))
    rel = jnp.arange(tk, dtype=jnp.int32)[None, :] - jnp.arange(tq, dtype=jnp.int32)[:, None]
    offs = jnp.arange(tk // tq, dtype=jnp.int32) * tq
    mask_bias = jnp.where(rel[None] <= offs[:, None, None], 0.0, NEG_BIG).astype(F32)
    kern = functools.partial(_diff_attn_kernel, tq=tq, tk=tk, dh=dh, lambda_init=lambda_init)
    vec = lambda a: a.reshape(1, -1).astype(F32)
    const2 = lambda h, i: (0, 0)
    vmem = (2 * (2 * s * hw * 2 + 2 * tq * hw * 2 + 3 * tk * hw * 2) + 4 * s * hw * 2
            + 8 * tq * tk * 4 + 2 * tq * (2 * hw + V7X_LANES) * 4 + 4 * MIB)
    return pl.pallas_call(
        kern,
        grid=(n_heads, s // tq),
        in_specs=[
            pl.BlockSpec((tq, hw), lambda h, i: (i, h)),
            pl.BlockSpec((s, hw), lambda h, i: (0, n_heads + h)),
            pl.BlockSpec((s, hw), lambda h, i: (0, 2 * n_heads + h)),
            pl.BlockSpec((2, tk, hw), lambda h, i: (0, 0, 0)),
            pl.BlockSpec((tk, hw), const2),
            pl.BlockSpec((1, tq, tk), lambda h, i: (i % (tk // tq), 0, 0)),
            pl.BlockSpec((1, 2, hw), lambda h, i: (h, 0, 0)),
            pl.BlockSpec((1, 1, V7X_LANES), lambda h, i: (h, 0, 0)),
            pl.BlockSpec((1, dh), const2),
            pl.BlockSpec((1, dh), const2),
            pl.BlockSpec((1, dh), const2),
            pl.BlockSpec((1, dh), const2),
            pl.BlockSpec((1, hw), const2),
        ],
        out_specs=pl.BlockSpec((tq, hw), lambda h, i: (i, h)),
        out_shape=jax.ShapeDtypeStruct((s, n_heads * hw), BF16),
        scratch_shapes=[
            pltpu.VMEM((2, s, hw), BF16),
            pltpu.VMEM((s, 2 * hw), BF16),
            pltpu.VMEM((2, tq, V7X_LANES), F32),
            pltpu.VMEM((2, tq, 2 * hw), F32),
            pltpu.VMEM((2, 2, tq, tk), F32),
        ],
        compiler_params=_cparams(("arbitrary", "arbitrary"), vmem),
        name="diff_attention",
    )(qkv, qkv, qkv, pos, ones, mask_bias, qc, shift, vec(lq1), vec(lk1), vec(lq2), vec(lk2), vec(subln_w))


def _rg_lru_kernel(rx_ref, rg_ref, cw_ref, cb_ref, wa_ref, ba_ref, wx_ref, bx_ref, lam_ref, o_ref,
                   ext_ref, a_ref, b_ref, tail_ref, h_ref, *, t_rows, cw, blk):
    t = pl.program_id(1)
    halo = V7X_SUBLANES

    @pl.when(t == 0)
    def _():
        tail_ref[...] = jnp.zeros(tail_ref.shape, F32)
        h_ref[...] = jnp.zeros(h_ref.shape, F32)

    rx = rx_ref[...]
    ext_ref[0:halo, :] = tail_ref[...]
    ext_ref[halo:, :] = rx
    tail_ref[...] = rx[t_rows - halo:, :]
    taps = cw_ref[...]
    n_taps = taps.shape[0]
    xr = cb_ref[...]
    for j in range(n_taps):
        lag = n_taps - 1 - j
        xs = rx if lag == 0 else ext_ref[halo - lag:halo - lag + t_rows, :]
        xr = xr + xs * taps[j:j + 1, :]

    ra, ia = [], []
    for g in range(cw // blk):
        xb = xr[:, g * blk:(g + 1) * blk].astype(BF16)
        ra.append(jnp.dot(xb, wa_ref[g], preferred_element_type=F32))
        ia.append(jnp.dot(xb, wx_ref[g], preferred_element_type=F32))
    r = jax.nn.sigmoid(jnp.concatenate(ra, axis=-1) + ba_ref[...])
    ig = jax.nn.sigmoid(jnp.concatenate(ia, axis=-1) + bx_ref[...])
    neg_lam = -lam_ref[...]
    softplus = jnp.maximum(neg_lam, 0.0) + jnp.log1p(jnp.exp(-jnp.abs(neg_lam)))
    log_a = (-RG_C) * r * softplus
    th = jnp.tanh(log_a)
    a_ref[...] = jnp.exp(log_a)
    b_ref[...] = jnp.sqrt(-2.0 * th / (1.0 - th)) * (ig * xr)

    row = lax.broadcasted_iota(jnp.int32, (V7X_SUBLANES, cw), 0)

    def step(n, h):
        rows = pl.ds(pl.multiple_of(n * V7X_SUBLANES, V7X_SUBLANES), V7X_SUBLANES)
        a = a_ref[rows, :]
        b = b_ref[rows, :]
        d = 1
        while d < V7X_SUBLANES:
            keep = row >= d
            a_s = jnp.where(keep, pltpu.roll(a, d, 0), 1.0)
            b_s = jnp.where(keep, pltpu.roll(b, d, 0), 0.0)
            b = a * b_s + b
            a = a * a_s
            d *= 2
        hh = a * h + b
        b_ref[rows, :] = hh
        return hh[V7X_SUBLANES - 1:, :]

    h_last = lax.fori_loop(0, t_rows // V7X_SUBLANES, step, h_ref[0:1, :], unroll=4)
    h_ref[...] = jnp.broadcast_to(h_last, h_ref.shape)
    o_ref[...] = (b_ref[...] * jax.nn.gelu(rg_ref[...])).astype(o_ref.dtype)


def _rg_lru(rest, conv_w, conv_b, w_a, b_a, w_x, b_x, lru_lambda, d_rnn):
    s = rest.shape[0]
    n_blk, blk, _ = w_a.shape
    cw = _tile(d_rnn, 512, blk)
    t_rows = _tile(s, 1024, 16)
    n_cb = d_rnn // cw
    row = lambda a: a.reshape(1, -1).astype(F32)
    kern = functools.partial(_rg_lru_kernel, t_rows=t_rows, cw=cw, blk=blk)
    vec_spec = pl.BlockSpec((1, cw), lambda n, t: (0, n))
    gate_spec = pl.BlockSpec((cw // blk, blk, blk), lambda n, t: (n, 0, 0))
    vmem = 2 * (2 * t_rows * cw * 4 + t_rows * cw * 2) + 12 * t_rows * cw * 4 + 4 * MIB
    return pl.pallas_call(
        kern,
        grid=(n_cb, s // t_rows),
        in_specs=[
            pl.BlockSpec((t_rows, cw), lambda n, t: (t, n)),
            pl.BlockSpec((t_rows, cw), lambda n, t: (t, n_cb + n)),
            pl.BlockSpec((conv_w.shape[0], cw), lambda n, t: (0, n)),
            vec_spec, gate_spec, vec_spec, gate_spec, vec_spec, vec_spec,
        ],
        out_specs=pl.BlockSpec((t_rows, cw), lambda n, t: (t, n)),
        out_shape=jax.ShapeDtypeStruct((s, d_rnn), BF16),
        scratch_shapes=[
            pltpu.VMEM((t_rows + V7X_SUBLANES, cw), F32),
            pltpu.VMEM((t_rows, cw), F32),
            pltpu.VMEM((t_rows, cw), F32),
            pltpu.VMEM((V7X_SUBLANES, cw), F32),
            pltpu.VMEM((V7X_SUBLANES, cw), F32),
        ],
        compiler_params=_cparams(("parallel", "arbitrary"), vmem),
        name="rg_lru",
    )(rest, rest, conv_w.astype(F32), row(conv_b), w_a.astype(BF16), row(b_a), w_x.astype(BF16), row(b_x),
      row(lru_lambda))


def _gated_merge_kernel(ya_ref, yr_ref, wa_ref, wr_ref, ga_ref, gb_ref, o_ref, wab_ref, wrb_ref):
    _cast_weight_once(wa_ref, wab_ref)
    _cast_weight_once(wr_ref, wrb_ref)
    pa = jnp.dot(ya_ref[...], wab_ref[...], preferred_element_type=F32)
    pr = jnp.dot(yr_ref[...], wrb_ref[...], preferred_element_type=F32)
    o_ref[...] = (jax.nn.sigmoid(ga_ref[...]) * pa + jax.nn.sigmoid(gb_ref[...]) * pr).astype(o_ref.dtype)


def _gated_merge(y_att, y_rec, w_pa, w_pr, rest, ga_col, gb_col):
    m, ka = y_att.shape
    _, kr = y_rec.shape
    n = w_pa.shape[1]
    tm = _tile(m, 1024, 16)
    tn = _tile(math.gcd(math.gcd(n, ga_col), gb_col), 512, V7X_LANES)
    vmem = (2 * (tm * (ka + kr) * 2 + (ka + kr) * tn * 4 + 2 * tm * tn * 4 + tm * tn * 2) + (ka + kr) * tn * 2
            + 4 * tm * tn * 4 + 2 * MIB)
    return pl.pallas_call(
        _gated_merge_kernel,
        grid=(n // tn, m // tm),
        in_specs=[
            pl.BlockSpec((tm, ka), lambda j, i: (i, 0)),
            pl.BlockSpec((tm, kr), lambda j, i: (i, 0)),
            pl.BlockSpec((ka, tn), lambda j, i: (0, j)),
            pl.BlockSpec((kr, tn), lambda j, i: (0, j)),
            pl.BlockSpec((tm, tn), lambda j, i: (i, ga_col // tn + j)),
            pl.BlockSpec((tm, tn), lambda j, i: (i, gb_col // tn + j)),
        ],
        out_specs=pl.BlockSpec((tm, tn), lambda j, i: (i, j)),
        out_shape=jax.ShapeDtypeStruct((m, n), BF16),
        scratch_shapes=[pltpu.VMEM((ka, tn), BF16), pltpu.VMEM((kr, tn), BF16)],
        compiler_params=_cparams(("arbitrary", "arbitrary"), vmem),
        name="gated_merge",
    )(y_att, y_rec, w_pa, w_pr, rest, rest)


def _matmul_residual_kernel(a_ref, w_ref, x_ref, g_ref, o_ref, wb_ref):
    _cast_weight_once(w_ref, wb_ref)
    y = jnp.dot(a_ref[...], wb_ref[...], preferred_element_type=F32)
    o_ref[...] = x_ref[...] + g_ref[...] * y


def _matmul_residual(a, w, x, mod, gate_idx):
    m, k = a.shape
    n = w.shape[1]
    tm = _tile(m, 1024, 16)
    tn = _tile(n, 512, V7X_LANES)
    vmem = 2 * (tm * k * 2 + k * tn * 4 + 2 * tm * tn * 4) + k * tn * 2 + tm * tn * 4 + 2 * MIB
    return pl.pallas_call(
        _matmul_residual_kernel,
        grid=(n // tn, m // tm),
        in_specs=[
            pl.BlockSpec((tm, k), lambda j, i: (i, 0)),
            pl.BlockSpec((k, tn), lambda j, i: (0, j)),
            pl.BlockSpec((tm, tn), lambda j, i: (i, j)),
            pl.BlockSpec((1, tn), lambda j, i: (0, gate_idx * (n // tn) + j)),
        ],
        out_specs=pl.BlockSpec((tm, tn), lambda j, i: (i, j)),
        out_shape=jax.ShapeDtypeStruct((m, n), F32),
        scratch_shapes=[pltpu.VMEM((k, tn), BF16)],
        compiler_params=_cparams(("arbitrary", "arbitrary"), vmem),
        name="matmul_residual",
    )(a, w, x, mod)


def _conv_ffn_kernel(h_ref, wg_ref, wu_ref, cw_ref, cb_ref, wd_ref, x_ref, g_ref, nw_ref, o_ref,
                     gate_ref, up_ref, act_ref, carry_ref, *, tm, dn, rn, rc, nf, final_norm):
    mi = pl.program_id(0)
    f = pl.program_id(1)
    halo = V7X_SUBLANES
    d = o_ref.shape[1]

    def gate_up(slot):
        @pl.when(mi == 0)
        def _():
            carry_ref[f] = jnp.zeros(carry_ref.shape[1:], F32)

        h = h_ref[...]
        gate = jnp.dot(h, wg_ref[...], preferred_element_type=F32)
        up_ref[slot] = jnp.dot(h, wu_ref[...], preferred_element_type=F32)
        gate_ref[slot, 0:halo, :] = carry_ref[f]
        gate_ref[slot, halo:, :] = gate
        carry_ref[f] = gate[tm - halo:, :]

    def activation_rows(slot, r):
        r0 = r * rc
        taps = cw_ref[...]
        n_taps = taps.shape[0]
        gc = cb_ref[...]
        for j in range(n_taps):
            start = halo - (n_taps - 1 - j) + r0
            gc = gc + gate_ref[slot, start:start + rc, :] * taps[j:j + 1, :]
        act_ref[slot, r0:r0 + rc, :] = (jax.nn.gelu(gc) * up_ref[slot, r0:r0 + rc, :]).astype(BF16)

    def down_cols(slot, c):
        cols = slice(c * dn, (c + 1) * dn)
        o_ref[:, cols] += jnp.dot(act_ref[slot], wd_ref[:, cols], preferred_element_type=F32)

    def step(gu_slot, act_slot, down_slot):
        n_c = d // dn
        n_r = tm // rc
        if gu_slot is not None:
            gate_up(gu_slot)
        for k in range(max(n_c, n_r)):
            if act_slot is not None and k < n_r:
                activation_rows(act_slot, k)
            if down_slot is not None and k < n_c:
                down_cols(down_slot, k)

    @pl.when(f == 0)
    def _():
        o_ref[...] = jnp.zeros(o_ref.shape, F32)
        step(0, None, None)

    @pl.when(f == 1)
    def _():
        step(1, 0, None)

    @pl.when(jnp.logical_and(jnp.logical_and(f >= 2, f < nf), f % 2 == 0))
    def _():
        step(0, 1, 0)

    @pl.when(jnp.logical_and(jnp.logical_and(f >= 2, f < nf), f % 2 == 1))
    def _():
        step(1, 0, 1)

    @pl.when(f == nf)
    def _():
        step(None, (nf - 1) % 2, nf % 2)

    @pl.when(f == nf + 1)
    def _():
        step(None, None, (nf - 1) % 2)
        for r in range(tm // rn):
            rows = slice(r * rn, (r + 1) * rn)
            x2 = x_ref[rows, :] + g_ref[...] * o_ref[rows, :]
            if final_norm:
                x2 = x2 * _rms_scale(x2) * nw_ref[...]
            o_ref[rows, :] = x2


def _conv_ffn(h, w_up, conv_w, conv_b, w_down, x, mod, gate_idx, norm_w, final_norm):
    s, d = h.shape
    d_ff = w_down.shape[0]
    tm = _tile(s, 512, 16)
    tf = _tile(d_ff, 256, V7X_LANES)
    nf = d_ff // tf
    assert nf >= 2
    dn = _tile(d, 512, V7X_LANES)
    rn = _tile(tm, 128, V7X_SUBLANES)
    rc = _tile(tm, 64, 16)
    kern = functools.partial(_conv_ffn_kernel, tm=tm, dn=dn, rn=rn, rc=rc, nf=nf, final_norm=final_norm)
    vmem = (2 * (tm * d * 2 + 2 * d * tf * 2 + tf * d * 2 + tm * d * 4) + tm * d * 4
            + 12 * tm * tf * 4 + 4 * rn * d * 4 + tm * dn * 4 + 2 * MIB)
    chunk = lambda f, lag: jnp.clip(f - lag, 0, nf - 1)
    return pl.pallas_call(
        kern,
        grid=(s // tm, nf + 2),
        in_specs=[
            pl.BlockSpec((tm, d), lambda i, f: (i, 0)),
            pl.BlockSpec((d, tf), lambda i, f: (0, chunk(f, 0))),
            pl.BlockSpec((d, tf), lambda i, f: (0, nf + chunk(f, 0))),
            pl.BlockSpec((conv_w.shape[0], tf), lambda i, f: (0, chunk(f, 1))),
            pl.BlockSpec((1, tf), lambda i, f: (0, chunk(f, 1))),
            pl.BlockSpec((tf, d), lambda i, f: (chunk(f, 2), 0)),
            pl.BlockSpec((tm, d), lambda i, f: (i, 0), pipeline_mode=pl.Buffered(1)),
            pl.BlockSpec((1, d), lambda i, f: (0, gate_idx)),
            pl.BlockSpec((1, d), lambda i, f: (0, 0)),
        ],
        out_specs=pl.BlockSpec((tm, d), lambda i, f: (i, 0)),
        out_shape=jax.ShapeDtypeStruct((s, d), F32),
        scratch_shapes=[
            pltpu.VMEM((2, tm + V7X_SUBLANES, tf), F32),
            pltpu.VMEM((2, tm, tf), F32),
            pltpu.VMEM((2, tm, tf), BF16),
            pltpu.VMEM((nf, V7X_SUBLANES, tf), F32),
        ],
        compiler_params=_cparams(("arbitrary", "arbitrary"), vmem),
        name="conv_ffn",
    )(h, w_up, w_up, conv_w.astype(F32), conv_b.reshape(1, -1).astype(F32), w_down, x, mod,
      norm_w.reshape(1, -1).astype(F32))


def kernel(x, c, w_ada, b_ada, norm1_w, w_in, lambda_q1, lambda_k1, lambda_q2, lambda_k2, subln_w, conv_w, conv_b, w_rg_a, b_rg_a, w_rg_x, b_rg_x, lru_lambda, w_proj_attn, w_proj_rec, w_out, norm2_w, w_ffn_up, ffn_conv_w, ffn_conv_b, w_ffn_down, norm_f_w):
    batch, seq, d = x.shape
    depth = w_ada.shape[0]
    dh = lambda_q1.shape[-1]
    attn_w = w_proj_attn.shape[1]
    n_heads = attn_w // (2 * dh)
    d_rnn = conv_w.shape[-1]
    assert w_in.shape[-1] == 3 * attn_w + 2 * d_rnn + 2 * d
    row = lambda a: a.reshape(1, -1).astype(F32)

    outs = []
    for b in range(batch):
        xb = x[b]
        cb = c[b:b + 1]
        for l in range(depth):
            lambda_init = 0.8 - 0.6 * math.exp(-0.3 * l)
            mod = _adaln_mod(cb, w_ada[l], row(b_ada[l]))
            h1 = _norm_modulate(xb, row(norm1_w[l]), mod, 0, 1)
            q_scale = jnp.concatenate([jnp.full((1, attn_w), dh ** -0.5 * LOG2E, F32),
                                       jnp.ones((1, 2 * attn_w), F32)], axis=1)
            qkv = _matmul(h1, w_in[l], 0, 3 * attn_w, BF16, q_scale)
            rest = _matmul(h1, w_in[l], 3 * attn_w, 2 * d_rnn + 2 * d, F32)
            y_att = _diff_attention(qkv, lambda_q1[l], lambda_k1[l], lambda_q2[l], lambda_k2[l], subln_w[l],
                                    lambda_init, n_heads, dh)
            y_rec = _rg_lru(rest, conv_w[l], conv_b[l], w_rg_a[l], b_rg_a[l], w_rg_x[l], b_rg_x[l],
                            lru_lambda[l], d_rnn)
            mixed = _gated_merge(y_att, y_rec, w_proj_attn[l], w_proj_rec[l], rest, 2 * d_rnn, 2 * d_rnn + d)
            x1 = _matmul_residual(mixed, w_out[l], xb, mod, 2)
            h2 = _norm_modulate(x1, row(norm2_w[l]), mod, 3, 4)
            last = l == depth - 1
            xb = _conv_ffn(h2, w_ffn_up[l].astype(BF16), ffn_conv_w[l], ffn_conv_b[l], w_ffn_down[l].astype(BF16),
                           x1, mod, 5, norm_f_w, final_norm=last)
        outs.append(xb)
    return outs[0][None] if batch == 1 else jnp.stack(outs, axis=0)
```

```python
import functools
import math

import jax
import jax.numpy as jnp
from jax import lax
from jax.experimental import pallas as pl
from jax.experimental.pallas import tpu as pltpu

F32 = jnp.float32
BF16 = jnp.bfloat16
EPS = 1e-6
RG_C = 8.0
NEG_BIG = -1e30

V7X_LANES = 128
V7X_SUBLANES = 8
V7X_VMEM_LIMIT_BYTES = 60000 * 1024
MIB = 1024 * 1024


def _cparams(semantics, vmem_bytes):
    return pltpu.CompilerParams(
        dimension_semantics=semantics,
        vmem_limit_bytes=int(min(vmem_bytes, V7X_VMEM_LIMIT_BYTES)),
    )


def _tile(n, want, quantum):
    t = min(n, want) // quantum * quantum
    while t > quantum and n % t:
        t -= quantum
    assert t >= quantum and n % t == 0, (n, want, quantum)
    return t


def _rms_scale(x):
    return lax.rsqrt(jnp.mean(x * x, axis=-1, keepdims=True) + EPS)


def _adaln_mod_kernel(c_ref, w_ref, b_ref, o_ref):
    c = c_ref[...]
    a = c * jax.nn.sigmoid(c)
    a8 = jnp.broadcast_to(a, (V7X_SUBLANES, a.shape[1])).astype(BF16)
    acc = jnp.dot(a8, w_ref[...].astype(BF16), preferred_element_type=F32)
    o_ref[...] = acc[0:1, :] + b_ref[...]


def _adaln_mod(c, w, b):
    d, n = w.shape
    tn = _tile(n, 1024, V7X_LANES)
    return pl.pallas_call(
        _adaln_mod_kernel,
        grid=(n // tn,),
        in_specs=[
            pl.BlockSpec((1, d), lambda j: (0, 0)),
            pl.BlockSpec((d, tn), lambda j: (0, j)),
            pl.BlockSpec((1, tn), lambda j: (0, j)),
        ],
        out_specs=pl.BlockSpec((1, tn), lambda j: (0, j)),
        out_shape=jax.ShapeDtypeStruct((1, n), F32),
        compiler_params=_cparams(("arbitrary",), 2 * d * tn * 4 + d * tn * 2 + 4 * MIB),
        name="adaln_mod",
    )(c, w, b)


def _norm_modulate_kernel(x_ref, w_ref, scale_ref, shift_ref, o_ref):
    x = x_ref[...]
    y = x * _rms_scale(x) * w_ref[...]
    o_ref[...] = (y * (1.0 + scale_ref[...]) + shift_ref[...]).astype(o_ref.dtype)


def _norm_modulate(x, w, mod, shift_idx, scale_idx):
    s, d = x.shape
    tm = _tile(s, 256, 16)
    return pl.pallas_call(
        _norm_modulate_kernel,
        grid=(s // tm,),
        in_specs=[
            pl.BlockSpec((tm, d), lambda i: (i, 0)),
            pl.BlockSpec((1, d), lambda i: (0, 0)),
            pl.BlockSpec((1, d), lambda i: (0, scale_idx)),
            pl.BlockSpec((1, d), lambda i: (0, shift_idx)),
        ],
        out_specs=pl.BlockSpec((tm, d), lambda i: (i, 0)),
        out_shape=jax.ShapeDtypeStruct((s, d), BF16),
        compiler_params=_cparams(("parallel",), 2 * tm * d * 6 + 3 * tm * d * 4 + 2 * MIB),
        name="norm_modulate",
    )(x, w, mod, mod)


def _cast_weight_once(w_ref, wb_ref):
    @pl.when(pl.program_id(1) == 0)
    def _():
        wb_ref[...] = w_ref[...].astype(BF16)


def _matmul_kernel(x_ref, w_ref, o_ref, wb_ref):
    _cast_weight_once(w_ref, wb_ref)
    o_ref[...] = jnp.dot(x_ref[...], wb_ref[...], preferred_element_type=F32).astype(o_ref.dtype)


def _matmul_colscale_kernel(x_ref, w_ref, s_ref, o_ref, wb_ref):
    _cast_weight_once(w_ref, wb_ref)
    y = jnp.dot(x_ref[...], wb_ref[...], preferred_element_type=F32)
    o_ref[...] = (y * s_ref[...]).astype(o_ref.dtype)


def _matmul(x, w, col0, n, out_dtype, col_scale=None):
    m, k = x.shape
    tm = _tile(m, 1024, 16)
    tn = _tile(math.gcd(n, col0) if col0 else n, 512, V7X_LANES)
    ob = jnp.dtype(out_dtype).itemsize
    vmem = 2 * (tm * k * 2 + k * tn * 4 + tm * tn * ob) + k * tn * 2 + 2 * tm * tn * 4 + 2 * MIB
    in_specs = [
        pl.BlockSpec((tm, k), lambda j, i: (i, 0)),
        pl.BlockSpec((k, tn), lambda j, i: (0, col0 // tn + j)),
    ]
    args = (x, w)
    if col_scale is not None:
        in_specs.append(pl.BlockSpec((1, tn), lambda j, i: (0, j)))
        args = (x, w, col_scale)
    return pl.pallas_call(
        _matmul_kernel if col_scale is None else _matmul_colscale_kernel,
        grid=(n // tn, m // tm),
        in_specs=in_specs,
        out_specs=pl.BlockSpec((tm, tn), lambda j, i: (i, j)),
        out_shape=jax.ShapeDtypeStruct((m, n), out_dtype),
        scratch_shapes=[pltpu.VMEM((k, tn), BF16)],
        compiler_params=_cparams(("arbitrary", "arbitrary"), vmem),
        name="matmul",
    )(*args)


LOG2E = math.log2(math.e)
POS_RADIX = 32
N_BIAS_COLS = 6


def _split3_bf16(x):
    hi = x.astype(BF16).astype(F32)
    mid = (x - hi).astype(BF16).astype(F32)
    lo = (x - hi - mid).astype(BF16).astype(F32)
    return hi, mid, lo


def _diff_attn_kernel(q_ref, k_ref, v_ref, pos_ref, ones_ref, mb_ref, qc_ref, sh_ref, lq1_ref, lk1_ref, lq2_ref, lk2_ref,
                      sw_ref, o_ref, ka_ref, va_ref, m_ref, acc_ref, z_ref, *, tq, tk, dh, lambda_init):
    i = pl.program_id(1)
    hw = 2 * dh
    s_len = k_ref.shape[0]

    @pl.when(i == 0)
    def _():
        lane_k = lax.broadcasted_iota(jnp.int32, (tk, hw), 1)

        def prep(j, carry):
            rows = pl.ds(pl.multiple_of(j * tk, tk), tk)
            kj = k_ref[rows, :]
            ka_ref[0, rows, :] = jnp.where(lane_k < dh, kj, pos_ref[0])
            ka_ref[1, rows, :] = jnp.where(lane_k >= dh, kj, pos_ref[1])
            va_ref[rows, 0:hw] = v_ref[rows, :]
            va_ref[rows, hw:2 * hw] = ones_ref[...]
            return carry

        lax.fori_loop(0, s_len // tk, prep, 0)

    q = q_ref[...]
    lane_q = lax.broadcasted_iota(jnp.int32, q.shape, 1)
    qc = qc_ref[0].astype(BF16)
    q_maps = (jnp.where(lane_q < dh, q, qc[0:1, :]), jnp.where(lane_q >= dh, q, qc[1:2, :]))
    frame_shift = sh_ref[0][:, 0:1]

    m_ref[...] = jnp.full(m_ref.shape, NEG_BIG, F32)
    acc_ref[...] = jnp.zeros(acc_ref.shape, F32)

    def scores(j, c):
        rows = pl.ds(pl.multiple_of(j * tk, tk), tk)
        return lax.dot_general(q_maps[c], ka_ref[c, rows, :], (((1,), (1,)), ((), ())),
                               preferred_element_type=F32)

    def lanes(x, width):
        return jnp.concatenate([x] * (width // V7X_LANES), axis=-1)

    def update(j, c, z):
        width = z.shape[1]
        rows = pl.ds(pl.multiple_of(j * tk, tk), width)
        m_prev = m_ref[c] - frame_shift
        m_new = jnp.maximum(m_prev, jnp.max(z, axis=-1, keepdims=True))
        alpha = jnp.exp2(m_prev - m_new)
        p = jnp.exp2(z - lanes(m_new, width)).astype(BF16)
        acc_ref[c] = lanes(alpha, 2 * hw) * acc_ref[c] + jnp.dot(p, va_ref[rows, :], preferred_element_type=F32)
        m_ref[c] = m_new

    n_full = (i * tq) // tk

    def step(j, cur):
        z_ref[1 - cur, 0] = scores(j + 1, 0)
        z_ref[1 - cur, 1] = scores(j + 1, 1)
        update(j, 0, z_ref[cur, 0])
        update(j, 1, z_ref[cur, 1])

    def masked_step(cur):
        for pos in range(tk // tq):
            @pl.when(i % (tk // tq) == pos)
            def _():
                width = (pos + 1) * tq
                causal = mb_ref[0, :, 0:width]
                update(n_full, 0, z_ref[cur, 0, :, 0:width] + causal)
                update(n_full, 1, z_ref[cur, 1, :, 0:width] + causal)

    z_ref[0, 0] = scores(0, 0)
    z_ref[0, 1] = scores(0, 1)

    def chunk_pair(t, carry):
        step(2 * t, 0)
        step(2 * t + 1, 1)
        return carry

    lax.fori_loop(0, n_full // 2, chunk_pair, 0)

    @pl.when(n_full % 2 == 1)
    def _():
        step(n_full - 1, 0)
        masked_step(1)

    @pl.when(n_full % 2 == 0)
    def _():
        masked_step(0)

    a0 = acc_ref[0]
    a1 = acc_ref[1]
    o1 = a0[:, 0:hw] / a0[:, hw:hw + 1]
    o2 = a1[:, 0:hw] / a1[:, hw:hw + 1]
    lam = (jnp.exp(jnp.sum(lq1_ref[...] * lk1_ref[...], axis=-1, keepdims=True))
           - jnp.exp(jnp.sum(lq2_ref[...] * lk2_ref[...], axis=-1, keepdims=True)) + lambda_init)
    o = o1 - lam * o2
    y = o * _rms_scale(o) * sw_ref[...]
    o_ref[...] = (y * (1.0 - lambda_init)).astype(o_ref.dtype)


def _diff_attention(qkv, lq1, lk1, lq2, lk2, subln_w, lambda_init, n_heads, dh):
    s = qkv.shape[0]
    hw = 2 * dh
    tk = _tile(s, 1024, V7X_LANES)
    tq = _tile(tk, 512, 16)
    nb = N_BIAS_COLS
    assert dh >= nb and tk // POS_RADIX <= 256
    slopes2 = (2.0 ** (-8.0 * jnp.arange(1, n_heads + 1, dtype=F32) / n_heads)) * LOG2E
    pieces = jnp.stack(_split3_bf16(slopes2), axis=-1)
    q_cols = jnp.concatenate([pieces * POS_RADIX, pieces], axis=-1)
    qc = jnp.zeros((n_heads, 2, hw), F32).at[:, 0, dh:dh + nb].set(q_cols).at[:, 1, 0:nb].set(q_cols)
    jj = jnp.arange(tk, dtype=jnp.int32)
    k_cols = jnp.stack([jj // POS_RADIX] * 3 + [jj % POS_RADIX] * 3, axis=-1).astype(F32)
    pos = jnp.zeros((2, tk, hw), F32).at[0, :, dh:dh + nb].set(k_cols).at[1, :, 0:nb].set(k_cols).astype(BF16)
    ones = jnp.zeros((tk, hw), F32).at[:, 0].set(1.0).astype(BF16)
    shift = jnp.broadcast_to((slopes2 * tk)[:, None, None], (n_heads, 1, V7X_LANES))
    rel = jnp.arange(tk, dtype=jnp.int32)[None, :] - jnp.arange(tq, dtype=jnp.int32)[:, None]
    offs = jnp.arange(tk // tq, dtype=jnp.int32) * tq
    mask_bias = jnp.where(rel[None] <= offs[:, None, None], 0.0, NEG_BIG).astype(F32)
    kern = functools.partial(_diff_attn_kernel, tq=tq, tk=tk, dh=dh, lambda_init=lambda_init)
    vec = lambda a: a.reshape(1, -1).astype(F32)
    const2 = lambda h, i: (0, 0)
    vmem = (2 * (2 * s * hw * 2 + 2 * tq * hw * 2 + 3 * tk * hw * 2) + 4 * s * hw * 2
            + 8 * tq * tk * 4 + 2 * tq * (2 * hw + V7X_LANES) * 4 + 4 * MIB)
    return pl.pallas_call(
        kern,
        grid=(n_heads, s // tq),
        in_specs=[
            pl.BlockSpec((tq, hw), lambda h, i: (i, h)),
            pl.BlockSpec((s, hw), lambda h, i: (0, n_heads + h)),
            pl.BlockSpec((s, hw), lambda h, i: (0, 2 * n_heads + h)),
            pl.BlockSpec((2, tk, hw), lambda h, i: (0, 0, 0)),
            pl.BlockSpec((tk, hw), const2),
            pl.BlockSpec((1, tq, tk), lambda h, i: (i % (tk // tq), 0, 0)),
            pl.BlockSpec((1, 2, hw), lambda h, i: (h, 0, 0)),
            pl.BlockSpec((1, 1, V7X_LANES), lambda h, i: (h, 0, 0)),
            pl.BlockSpec((1, dh), const2),
            pl.BlockSpec((1, dh), const2),
            pl.BlockSpec((1, dh), const2),
            pl.BlockSpec((1, dh), const2),
            pl.BlockSpec((1, hw), const2),
        ],
        out_specs=pl.BlockSpec((tq, hw), lambda h, i: (i, h)),
        out_shape=jax.ShapeDtypeStruct((s, n_heads * hw), BF16),
        scratch_shapes=[
            pltpu.VMEM((2, s, hw), BF16),
            pltpu.VMEM((s, 2 * hw), BF16),
            pltpu.VMEM((2, tq, V7X_LANES), F32),
            pltpu.VMEM((2, tq, 2 * hw), F32),
            pltpu.VMEM((2, 2, tq, tk), F32),
        ],
        compiler_params=_cparams(("arbitrary", "arbitrary"), vmem),
        name="diff_attention",
    )(qkv, qkv, qkv, pos, ones, mask_bias, qc, shift, vec(lq1), vec(lk1), vec(lq2), vec(lk2), vec(subln_w))


def _rg_lru_kernel(rx_ref, rg_ref, cw_ref, cb_ref, wa_ref, ba_ref, wx_ref, bx_ref, lam_ref, o_ref,
                   ext_ref, a_ref, b_ref, tail_ref, h_ref, *, t_rows, cw, blk):
    t = pl.program_id(1)
    halo = V7X_SUBLANES

    @pl.when(t == 0)
    def _():
        tail_ref[...] = jnp.zeros(tail_ref.shape, F32)
        h_ref[...] = jnp.zeros(h_ref.shape, F32)

    rx = rx_ref[...]
    ext_ref[0:halo, :] = tail_ref[...]
    ext_ref[halo:, :] = rx
    tail_ref[...] = rx[t_rows - halo:, :]
    taps = cw_ref[...]
    n_taps = taps.shape[0]
    xr = cb_ref[...]
    for j in range(n_taps):
        lag = n_taps - 1 - j
        xs = rx if lag == 0 else ext_ref[halo - lag:halo - lag + t_rows, :]
        xr = xr + xs * taps[j:j + 1, :]

    ra, ia = [], []
    for g in range(cw // blk):
        xb = xr[:, g * blk:(g + 1) * blk].astype(BF16)
        ra.append(jnp.dot(xb, wa_ref[g], preferred_element_type=F32))
        ia.append(jnp.dot(xb, wx_ref[g], preferred_element_type=F32))
    r = jax.nn.sigmoid(jnp.concatenate(ra, axis=-1) + ba_ref[...])
    ig = jax.nn.sigmoid(jnp.concatenate(ia, axis=-1) + bx_ref[...])
    neg_lam = -lam_ref[...]
    softplus = jnp.maximum(neg_lam, 0.0) + jnp.log1p(jnp.exp(-jnp.abs(neg_lam)))
    log_a = (-RG_C) * r * softplus
    th = jnp.tanh(log_a)
    a_ref[...] = jnp.exp(log_a)
    b_ref[...] = jnp.sqrt(-2.0 * th / (1.0 - th)) * (ig * xr)

    row = lax.broadcasted_iota(jnp.int32, (V7X_SUBLANES, cw), 0)

    def step(n, h):
        rows = pl.ds(pl.multiple_of(n * V7X_SUBLANES, V7X_SUBLANES), V7X_SUBLANES)
        a = a_ref[rows, :]
        b = b_ref[rows, :]
        d = 1
        while d < V7X_SUBLANES:
            keep = row >= d
            a_s = jnp.where(keep, pltpu.roll(a, d, 0), 1.0)
            b_s = jnp.where(keep, pltpu.roll(b, d, 0), 0.0)
            b = a * b_s + b
            a = a * a_s
            d *= 2
        hh = a * h + b
        b_ref[rows, :] = hh
        return hh[V7X_SUBLANES - 1:, :]

    h_last = lax.fori_loop(0, t_rows // V7X_SUBLANES, step, h_ref[0:1, :], unroll=4)
    h_ref[...] = jnp.broadcast_to(h_last, h_ref.shape)
    o_ref[...] = (b_ref[...] * jax.nn.gelu(rg_ref[...])).astype(o_ref.dtype)


def _rg_lru(rest, conv_w, conv_b, w_a, b_a, w_x, b_x, lru_lambda, d_rnn):
    s = rest.shape[0]
    n_blk, blk, _ = w_a.shape
    cw = _tile(d_rnn, 512, blk)
    t_rows = _tile(s, 1024, 16)
    n_cb = d_rnn // cw
    row = lambda a: a.reshape(1, -1).astype(F32)
    kern = functools.partial(_rg_lru_kernel, t_rows=t_rows, cw=cw, blk=blk)
    vec_spec = pl.BlockSpec((1, cw), lambda n, t: (0, n))
    gate_spec = pl.BlockSpec((cw // blk, blk, blk), lambda n, t: (n, 0, 0))
    vmem = 2 * (2 * t_rows * cw * 4 + t_rows * cw * 2) + 12 * t_rows * cw * 4 + 4 * MIB
    return pl.pallas_call(
        kern,
        grid=(n_cb, s // t_rows),
        in_specs=[
            pl.BlockSpec((t_rows, cw), lambda n, t: (t, n)),
            pl.BlockSpec((t_rows, cw), lambda n, t: (t, n_cb + n)),
            pl.BlockSpec((conv_w.shape[0], cw), lambda n, t: (0, n)),
            vec_spec, gate_spec, vec_spec, gate_spec, vec_spec, vec_spec,
        ],
        out_specs=pl.BlockSpec((t_rows, cw), lambda n, t: (t, n)),
        out_shape=jax.ShapeDtypeStruct((s, d_rnn), BF16),
        scratch_shapes=[
            pltpu.VMEM((t_rows + V7X_SUBLANES, cw), F32),
            pltpu.VMEM((t_rows, cw), F32),
            pltpu.VMEM((t_rows, cw), F32),
            pltpu.VMEM((V7X_SUBLANES, cw), F32),
            pltpu.VMEM((V7X_SUBLANES, cw), F32),
        ],
        compiler_params=_cparams(("parallel", "arbitrary"), vmem),
        name="rg_lru",
    )(rest, rest, conv_w.astype(F32), row(conv_b), w_a.astype(BF16), row(b_a), w_x.astype(BF16), row(b_x),
      row(lru_lambda))


def _gated_merge_kernel(ya_ref, yr_ref, wa_ref, wr_ref, ga_ref, gb_ref, o_ref, wab_ref, wrb_ref):
    _cast_weight_once(wa_ref, wab_ref)
    _cast_weight_once(wr_ref, wrb_ref)
    pa = jnp.dot(ya_ref[...], wab_ref[...], preferred_element_type=F32)
    pr = jnp.dot(yr_ref[...], wrb_ref[...], preferred_element_type=F32)
    o_ref[...] = (jax.nn.sigmoid(ga_ref[...]) * pa + jax.nn.sigmoid(gb_ref[...]) * pr).astype(o_ref.dtype)


def _gated_merge(y_att, y_rec, w_pa, w_pr, rest, ga_col, gb_col):
    m, ka = y_att.shape
    _, kr = y_rec.shape
    n = w_pa.shape[1]
    tm = _tile(m, 1024, 16)
    tn = _tile(math.gcd(math.gcd(n, ga_col), gb_col), 512, V7X_LANES)
    vmem = (2 * (tm * (ka + kr) * 2 + (ka + kr) * tn * 4 + 2 * tm * tn * 4 + tm * tn * 2) + (ka + kr) * tn * 2
            + 4 * tm * tn * 4 + 2 * MIB)
    return pl.pallas_call(
        _gated_merge_kernel,
        grid=(n // tn, m // tm),
        in_specs=[
            pl.BlockSpec((tm, ka), lambda j, i: (i, 0)),
            pl.BlockSpec((tm, kr), lambda j, i: (i, 0)),
            pl.BlockSpec((ka, tn), lambda j, i: (0, j)),
            pl.BlockSpec((kr, tn), lambda j, i: (0, j)),
            pl.BlockSpec((tm, tn), lambda j, i: (i, ga_col // tn + j)),
            pl.BlockSpec((tm, tn), lambda j, i: (i, gb_col // tn + j)),
        ],
        out_specs=pl.BlockSpec((tm, tn), lambda j, i: (i, j)),
        out_shape=jax.ShapeDtypeStruct((m, n), BF16),
        scratch_shapes=[pltpu.VMEM((ka, tn), BF16), pltpu.VMEM((kr, tn), BF16)],
        compiler_params=_cparams(("arbitrary", "arbitrary"), vmem),
        name="gated_merge",
    )(y_att, y_rec, w_pa, w_pr, rest, rest)


def _matmul_residual_kernel(a_ref, w_ref, x_ref, g_ref, o_ref, wb_ref):
    _cast_weight_once(w_ref, wb_ref)
    y = jnp.dot(a_ref[...], wb_ref[...], preferred_element_type=F32)
    o_ref[...] = x_ref[...] + g_ref[...] * y


def _matmul_residual(a, w, x, mod, gate_idx):
    m, k = a.shape
    n = w.shape[1]
    tm = _tile(m, 1024, 16)
    tn = _tile(n, 512, V7X_LANES)
    vmem = 2 * (tm * k * 2 + k * tn * 4 + 2 * tm * tn * 4) + k * tn * 2 + tm * tn * 4 + 2 * MIB
    return pl.pallas_call(
        _matmul_residual_kernel,
        grid=(n // tn, m // tm),
        in_specs=[
            pl.BlockSpec((tm, k), lambda j, i: (i, 0)),
            pl.BlockSpec((k, tn), lambda j, i: (0, j)),
            pl.BlockSpec((tm, tn), lambda j, i: (i, j)),
            pl.BlockSpec((1, tn), lambda j, i: (0, gate_idx * (n // tn) + j)),
        ],
        out_specs=pl.BlockSpec((tm, tn), lambda j, i: (i, j)),
        out_shape=jax.ShapeDtypeStruct((m, n), F32),
        scratch_shapes=[pltpu.VMEM((k, tn), BF16)],
        compiler_params=_cparams(("arbitrary", "arbitrary"), vmem),
        name="matmul_residual",
    )(a, w, x, mod)


def _conv_ffn_kernel(h_ref, wg_ref, wu_ref, cw_ref, cb_ref, wd_ref, x_ref, g_ref, nw_ref, o_ref,
                     gate_ref, up_ref, act_ref, carry_ref, *, tm, dn, rn, rc, nf, final_norm):
    mi = pl.program_id(0)
    f = pl.program_id(1)
    halo = V7X_SUBLANES
    d = o_ref.shape[1]

    def gate_up(slot):
        @pl.when(mi == 0)
        def _():
            carry_ref[f] = jnp.zeros(carry_ref.shape[1:], F32)

        h = h_ref[...]
        gate = jnp.dot(h, wg_ref[0], preferred_element_type=F32)
        up_ref[slot] = jnp.dot(h, wu_ref[0], preferred_element_type=F32)
        gate_ref[slot, 0:halo, :] = carry_ref[f]
        gate_ref[slot, halo:, :] = gate
        carry_ref[f] = gate[tm - halo:, :]

    def activation_rows(slot, r):
        r0 = r * rc
        taps = cw_ref[...]
        n_taps = taps.shape[0]
        gc = cb_ref[...]
        for j in range(n_taps):
            start = halo - (n_taps - 1 - j) + r0
            gc = gc + gate_ref[slot, start:start + rc, :] * taps[j:j + 1, :]
        act_ref[slot, r0:r0 + rc, :] = (jax.nn.gelu(gc) * up_ref[slot, r0:r0 + rc, :]).astype(BF16)

    def down_cols(slot, c):
        cols = slice(c * dn, (c + 1) * dn)
        o_ref[:, cols] += jnp.dot(act_ref[slot], wd_ref[:, cols], preferred_element_type=F32)

    def step(gu_slot, act_slot, down_slot):
        n_c = d // dn
        n_r = tm // rc
        if gu_slot is not None:
            gate_up(gu_slot)
        for k in range(max(n_c, n_r)):
            if act_slot is not None and k < n_r:
                activation_rows(act_slot, k)
            if down_slot is not None and k < n_c:
                down_cols(down_slot, k)

    @pl.when(f == 0)
    def _():
        o_ref[...] = jnp.zeros(o_ref.shape, F32)
        step(0, None, None)

    @pl.when(f == 1)
    def _():
        step(1, 0, None)

    @pl.when(jnp.logical_and(jnp.logical_and(f >= 2, f < nf), f % 2 == 0))
    def _():
        step(0, 1, 0)

    @pl.when(jnp.logical_and(jnp.logical_and(f >= 2, f < nf), f % 2 == 1))
    def _():
        step(1, 0, 1)

    @pl.when(f == nf)
    def _():
        step(None, (nf - 1) % 2, nf % 2)

    @pl.when(f == nf + 1)
    def _():
        step(None, None, (nf - 1) % 2)
        for r in range(tm // rn):
            rows = slice(r * rn, (r + 1) * rn)
            x2 = x_ref[rows, :] + g_ref[...] * o_ref[rows, :]
            if final_norm:
                x2 = x2 * _rms_scale(x2) * nw_ref[...]
            o_ref[rows, :] = x2


def _conv_ffn(h, w_up, conv_w, conv_b, w_down, x, mod, gate_idx, norm_w, final_norm):
    s, d = h.shape
    d_ff = w_down.shape[0]
    tm = _tile(s, 512, 16)
    tf = _tile(d_ff, 256, V7X_LANES)
    nf = d_ff // tf
    assert nf >= 2
    w_up = w_up.reshape(d, 2 * nf, tf).transpose(1, 0, 2).astype(BF16)
    w_down = w_down.astype(BF16)
    dn = _tile(d, 512, V7X_LANES)
    rn = _tile(tm, 128, V7X_SUBLANES)
    rc = _tile(tm, 64, 16)
    kern = functools.partial(_conv_ffn_kernel, tm=tm, dn=dn, rn=rn, rc=rc, nf=nf, final_norm=final_norm)
    vmem = (2 * (tm * d * 2 + 2 * d * tf * 2 + tf * d * 2 + tm * d * 4) + tm * d * 4
            + 12 * tm * tf * 4 + 4 * rn * d * 4 + tm * dn * 4 + 2 * MIB)
    chunk = lambda f, lag: jnp.clip(f - lag, 0, nf - 1)
    return pl.pallas_call(
        kern,
        grid=(s // tm, nf + 2),
        in_specs=[
            pl.BlockSpec((tm, d), lambda i, f: (i, 0)),
            pl.BlockSpec((1, d, tf), lambda i, f: (chunk(f, 0), 0, 0)),
            pl.BlockSpec((1, d, tf), lambda i, f: (nf + chunk(f, 0), 0, 0)),
            pl.BlockSpec((conv_w.shape[0], tf), lambda i, f: (0, chunk(f, 1))),
            pl.BlockSpec((1, tf), lambda i, f: (0, chunk(f, 1))),
            pl.BlockSpec((tf, d), lambda i, f: (chunk(f, 2), 0)),
            pl.BlockSpec((tm, d), lambda i, f: (i, 0), pipeline_mode=pl.Buffered(1)),
            pl.BlockSpec((1, d), lambda i, f: (0, gate_idx)),
            pl.BlockSpec((1, d), lambda i, f: (0, 0)),
        ],
        out_specs=pl.BlockSpec((tm, d), lambda i, f: (i, 0)),
        out_shape=jax.ShapeDtypeStruct((s, d), F32),
        scratch_shapes=[
            pltpu.VMEM((2, tm + V7X_SUBLANES, tf), F32),
            pltpu.VMEM((2, tm, tf), F32),
            pltpu.VMEM((2, tm, tf), BF16),
            pltpu.VMEM((nf, V7X_SUBLANES, tf), F32),
        ],
        compiler_params=_cparams(("arbitrary", "arbitrary"), vmem),
        name="conv_ffn",
    )(h, w_up, w_up, conv_w.astype(F32), conv_b.reshape(1, -1).astype(F32), w_down, x, mod,
      norm_w.reshape(1, -1).astype(F32))


def kernel(x, c, w_ada, b_ada, norm1_w, w_in, lambda_q1, lambda_k1, lambda_q2, lambda_k2, subln_w, conv_w, conv_b, w_rg_a, b_rg_a, w_rg_x, b_rg_x, lru_lambda, w_proj_attn, w_proj_rec, w_out, norm2_w, w_ffn_up, ffn_conv_w, ffn_conv_b, w_ffn_down, norm_f_w):
    batch, seq, d = x.shape
    depth = w_ada.shape[0]
    dh = lambda_q1.shape[-1]
    attn_w = w_proj_attn.shape[1]
    n_heads = attn_w // (2 * dh)
    d_rnn = conv_w.shape[-1]
    assert w_in.shape[-1] == 3 * attn_w + 2 * d_rnn + 2 * d
    row = lambda a: a.reshape(1, -1).astype(F32)

    outs = []
    for b in range(batch):
        xb = x[b]
        cb = c[b:b + 1]
        for l in range(depth):
            lambda_init = 0.8 - 0.6 * math.exp(-0.3 * l)
            mod = _adaln_mod(cb, w_ada[l], row(b_ada[l]))
            h1 = _norm_modulate(xb, row(norm1_w[l]), mod, 0, 1)
            q_scale = jnp.concatenate([jnp.full((1, attn_w), dh ** -0.5 * LOG2E, F32),
                                       jnp.ones((1, 2 * attn_w), F32)], axis=1)
            qkv = _matmul(h1, w_in[l], 0, 3 * attn_w, BF16, q_scale)
            rest = _matmul(h1, w_in[l], 3 * attn_w, 2 * d_rnn + 2 * d, F32)
            y_att = _diff_attention(qkv, lambda_q1[l], lambda_k1[l], lambda_q2[l], lambda_k2[l], subln_w[l],
                                    lambda_init, n_heads, dh)
            y_rec = _rg_lru(rest, conv_w[l], conv_b[l], w_rg_a[l], b_rg_a[l], w_rg_x[l], b_rg_x[l],
                            lru_lambda[l], d_rnn)
            mixed = _gated_merge(y_att, y_rec, w_proj_attn[l], w_proj_rec[l], rest, 2 * d_rnn, 2 * d_rnn + d)
            x1 = _matmul_residual(mixed, w_out[l], xb, mod, 2)
            h2 = _norm_modulate(x1, row(norm2_w[l]), mod, 3, 4)
            last = l == depth - 1
            xb = _conv_ffn(h2, w_ffn_up[l], ffn_conv_w[l], ffn_conv_b[l], w_ffn_down[l], x1, mod, 5, norm_f_w,
                           final_norm=last)
        outs.append(xb)
    return outs[0][None] if batch == 1 else jnp.stack(outs, axis=0)
```

```python
import functools
import math

import jax
import jax.numpy as jnp
from jax import lax
from jax.experimental import pallas as pl
from jax.experimental.pallas import tpu as pltpu

F32 = jnp.float32
BF16 = jnp.bfloat16
EPS = 1e-6
RG_C = 8.0
NEG_BIG = -1e30

V7X_LANES = 128
V7X_SUBLANES = 8
V7X_VMEM_LIMIT_BYTES = 60000 * 1024
MIB = 1024 * 1024


def _cparams(semantics, vmem_bytes):
    return pltpu.CompilerParams(
        dimension_semantics=semantics,
        vmem_limit_bytes=int(min(vmem_bytes, V7X_VMEM_LIMIT_BYTES)),
    )


def _tile(n, want, quantum):
    t = min(n, want) // quantum * quantum
    while t > quantum and n % t:
        t -= quantum
    assert t >= quantum and n % t == 0, (n, want, quantum)
    return t


def _rms_scale(x):
    return lax.rsqrt(jnp.mean(x * x, axis=-1, keepdims=True) + EPS)


def _adaln_mod_kernel(c_ref, w_ref, b_ref, o_ref):
    c = c_ref[...]
    a = c * jax.nn.sigmoid(c)
    a8 = jnp.broadcast_to(a, (V7X_SUBLANES, a.shape[1])).astype(BF16)
    acc = jnp.dot(a8, w_ref[...].astype(BF16), preferred_element_type=F32)
    o_ref[...] = acc[0:1, :] + b_ref[...]


def _adaln_mod(c, w, b):
    d, n = w.shape
    tn = _tile(n, 1024, V7X_LANES)
    return pl.pallas_call(
        _adaln_mod_kernel,
        grid=(n // tn,),
        in_specs=[
            pl.BlockSpec((1, d), lambda j: (0, 0)),
            pl.BlockSpec((d, tn), lambda j: (0, j)),
            pl.BlockSpec((1, tn), lambda j: (0, j)),
        ],
        out_specs=pl.BlockSpec((1, tn), lambda j: (0, j)),
        out_shape=jax.ShapeDtypeStruct((1, n), F32),
        compiler_params=_cparams(("arbitrary",), 2 * d * tn * 4 + d * tn * 2 + 4 * MIB),
        name="adaln_mod",
    )(c, w, b)


def _norm_modulate_kernel(x_ref, w_ref, scale_ref, shift_ref, o_ref):
    x = x_ref[...]
    y = x * _rms_scale(x) * w_ref[...]
    o_ref[...] = (y * (1.0 + scale_ref[...]) + shift_ref[...]).astype(o_ref.dtype)


def _norm_modulate(x, w, mod, shift_idx, scale_idx):
    s, d = x.shape
    tm = _tile(s, 256, 16)
    return pl.pallas_call(
        _norm_modulate_kernel,
        grid=(s // tm,),
        in_specs=[
            pl.BlockSpec((tm, d), lambda i: (i, 0)),
            pl.BlockSpec((1, d), lambda i: (0, 0)),
            pl.BlockSpec((1, d), lambda i: (0, scale_idx)),
            pl.BlockSpec((1, d), lambda i: (0, shift_idx)),
        ],
        out_specs=pl.BlockSpec((tm, d), lambda i: (i, 0)),
        out_shape=jax.ShapeDtypeStruct((s, d), BF16),
        compiler_params=_cparams(("parallel",), 2 * tm * d * 6 + 3 * tm * d * 4 + 2 * MIB),
        name="norm_modulate",
    )(x, w, mod, mod)


def _cast_weight_once(w_ref, wb_ref):
    @pl.when(pl.program_id(1) == 0)
    def _():
        wb_ref[...] = w_ref[...].astype(BF16)


def _matmul_kernel(x_ref, w_ref, o_ref, wb_ref):
    _cast_weight_once(w_ref, wb_ref)
    o_ref[...] = jnp.dot(x_ref[...], wb_ref[...], preferred_element_type=F32).astype(o_ref.dtype)


def _matmul_colscale_kernel(x_ref, w_ref, s_ref, o_ref, wb_ref):
    _cast_weight_once(w_ref, wb_ref)
    y = jnp.dot(x_ref[...], wb_ref[...], preferred_element_type=F32)
    o_ref[...] = (y * s_ref[...]).astype(o_ref.dtype)


def _matmul(x, w, col0, n, out_dtype, col_scale=None):
    m, k = x.shape
    tm = _tile(m, 1024, 16)
    tn = _tile(math.gcd(n, col0) if col0 else n, 512, V7X_LANES)
    ob = jnp.dtype(out_dtype).itemsize
    vmem = 2 * (tm * k * 2 + k * tn * 4 + tm * tn * ob) + k * tn * 2 + 2 * tm * tn * 4 + 2 * MIB
    in_specs = [
        pl.BlockSpec((tm, k), lambda j, i: (i, 0)),
        pl.BlockSpec((k, tn), lambda j, i: (0, col0 // tn + j)),
    ]
    args = (x, w)
    if col_scale is not None:
        in_specs.append(pl.BlockSpec((1, tn), lambda j, i: (0, j)))
        args = (x, w, col_scale)
    return pl.pallas_call(
        _matmul_kernel if col_scale is None else _matmul_colscale_kernel,
        grid=(n // tn, m // tm),
        in_specs=in_specs,
        out_specs=pl.BlockSpec((tm, tn), lambda j, i: (i, j)),
        out_shape=jax.ShapeDtypeStruct((m, n), out_dtype),
        scratch_shapes=[pltpu.VMEM((k, tn), BF16)],
        compiler_params=_cparams(("arbitrary", "arbitrary"), vmem),
        name="matmul",
    )(*args)


LOG2E = math.log2(math.e)
POS_RADIX = 32
N_BIAS_COLS = 6


def _split3_bf16(x):
    hi = x.astype(BF16).astype(F32)
    mid = (x - hi).astype(BF16).astype(F32)
    lo = (x - hi - mid).astype(BF16).astype(F32)
    return hi, mid, lo


def _diff_attn_kernel(q_ref, k_ref, v_ref, pos_ref, ones_ref, tri_ref, qc_ref, sh_ref, lq1_ref, lk1_ref, lq2_ref, lk2_ref,
                      sw_ref, o_ref, ka_ref, va_ref, m_ref, acc_ref, z_ref, *, tq, tk, dh, lambda_init):
    i = pl.program_id(1)
    hw = 2 * dh
    s_len = k_ref.shape[0]

    @pl.when(i == 0)
    def _():
        lane_k = lax.broadcasted_iota(jnp.int32, (tk, hw), 1)

        def prep(j, carry):
            rows = pl.ds(pl.multiple_of(j * tk, tk), tk)
            kj = k_ref[rows, :]
            ka_ref[0, rows, :] = jnp.where(lane_k < dh, kj, pos_ref[0])
            ka_ref[1, rows, :] = jnp.where(lane_k >= dh, kj, pos_ref[1])
            va_ref[rows, 0:hw] = v_ref[rows, :]
            va_ref[rows, hw:2 * hw] = ones_ref[...]
            return carry

        lax.fori_loop(0, s_len // tk, prep, 0)

    lane_q = lax.broadcasted_iota(jnp.int32, (tq, hw), 1)
    qc = qc_ref[0].astype(BF16)

    def score_operands(q):
        return (jnp.where(lane_q < dh, q, qc[0:1, :]), jnp.where(lane_q >= dh, q, qc[1:2, :]))

    q_maps = score_operands(q_ref[...])
    frame_shift = sh_ref[0][:, 0:1]

    m_ref[...] = jnp.full(m_ref.shape, NEG_BIG, F32)
    acc_ref[...] = jnp.zeros(acc_ref.shape, F32)

    def scores(maps, j, c):
        rows = pl.ds(pl.multiple_of(j * tk, tk), tk)
        return lax.dot_general(maps[c], ka_ref[c, rows, :], (((1,), (1,)), ((), ())),
                               preferred_element_type=F32)

    def lanes(x, width):
        return jnp.concatenate([x] * (width // V7X_LANES), axis=-1)

    def update(j, c, z):
        width = z.shape[1]
        rows = pl.ds(pl.multiple_of(j * tk, tk), width)
        m_prev = m_ref[c] - frame_shift
        m_new = jnp.maximum(m_prev, jnp.max(z, axis=-1, keepdims=True))
        alpha = jnp.exp2(m_prev - m_new)
        p = jnp.exp2(z - lanes(m_new, width)).astype(BF16)
        acc_ref[c] = lanes(alpha, 2 * hw) * acc_ref[c] + jnp.dot(p, va_ref[rows, :], preferred_element_type=F32)
        m_ref[c] = m_new

    n_full = (i * tq) // tk

    def step(j, cur):
        for c in range(2):
            z_ref[1 - cur, c] = scores(q_maps, j + 1, c)
        for c in range(2):
            update(j, c, z_ref[cur, c])

    def diagonal_step(cur):
        for pos in range(tk // tq):
            @pl.when(i % (tk // tq) == pos)
            def _():
                lo, width = pos * tq, (pos + 1) * tq
                for c in range(2):
                    diag = z_ref[cur, c, :, lo:width] + tri_ref[...]
                    z = diag if pos == 0 else jnp.concatenate([z_ref[cur, c, :, 0:lo], diag], axis=-1)
                    update(n_full, c, z)

    for c in range(2):
        z_ref[0, c] = scores(q_maps, 0, c)

    def chunk_pair(t, carry):
        step(2 * t, 0)
        step(2 * t + 1, 1)
        return carry

    lax.fori_loop(0, n_full // 2, chunk_pair, 0)

    @pl.when(n_full % 2 == 1)
    def _():
        step(n_full - 1, 0)
        diagonal_step(1)

    @pl.when(n_full % 2 == 0)
    def _():
        diagonal_step(0)

    a0 = acc_ref[0]
    a1 = acc_ref[1]
    o1 = a0[:, 0:hw] / a0[:, hw:hw + 1]
    o2 = a1[:, 0:hw] / a1[:, hw:hw + 1]
    lam = (jnp.exp(jnp.sum(lq1_ref[...] * lk1_ref[...], axis=-1, keepdims=True))
           - jnp.exp(jnp.sum(lq2_ref[...] * lk2_ref[...], axis=-1, keepdims=True)) + lambda_init)
    o = o1 - lam * o2
    y = o * _rms_scale(o) * sw_ref[...]
    o_ref[...] = (y * (1.0 - lambda_init)).astype(o_ref.dtype)


def _diff_attention(qkv, lq1, lk1, lq2, lk2, subln_w, lambda_init, n_heads, dh):
    s = qkv.shape[0]
    hw = 2 * dh
    tk = _tile(s, 1024, V7X_LANES)
    tq = _tile(tk, 512, 16)
    nb = N_BIAS_COLS
    assert dh >= nb and tk // POS_RADIX <= 256
    slopes2 = (2.0 ** (-8.0 * jnp.arange(1, n_heads + 1, dtype=F32) / n_heads)) * LOG2E
    pieces = jnp.stack(_split3_bf16(slopes2), axis=-1)
    q_cols = jnp.concatenate([pieces * POS_RADIX, pieces], axis=-1)
    qc = jnp.zeros((n_heads, 2, hw), F32).at[:, 0, dh:dh + nb].set(q_cols).at[:, 1, 0:nb].set(q_cols)
    jj = jnp.arange(tk, dtype=jnp.int32)
    k_cols = jnp.stack([jj // POS_RADIX] * 3 + [jj % POS_RADIX] * 3, axis=-1).astype(F32)
    pos = jnp.zeros((2, tk, hw), F32).at[0, :, dh:dh + nb].set(k_cols).at[1, :, 0:nb].set(k_cols).astype(BF16)
    ones = jnp.zeros((tk, hw), F32).at[:, 0].set(1.0).astype(BF16)
    shift = jnp.broadcast_to((slopes2 * tk)[:, None, None], (n_heads, 1, V7X_LANES))
    rel = jnp.arange(tq, dtype=jnp.int32)[None, :] - jnp.arange(tq, dtype=jnp.int32)[:, None]
    tri_bias = jnp.where(rel <= 0, 0.0, NEG_BIG).astype(F32)
    kern = functools.partial(_diff_attn_kernel, tq=tq, tk=tk, dh=dh, lambda_init=lambda_init)
    vec = lambda a: a.reshape(1, -1).astype(F32)
    const2 = lambda h, i: (0, 0)
    vmem = (2 * (2 * s * hw * 2 + 2 * tq * hw * 2 + 3 * tk * hw * 2) + 4 * s * hw * 2
            + 8 * tq * tk * 4 + 2 * tq * (2 * hw + V7X_LANES) * 4 + 4 * MIB)
    return pl.pallas_call(
        kern,
        grid=(n_heads, s // tq),
        in_specs=[
            pl.BlockSpec((tq, hw), lambda h, i: (i, h)),
            pl.BlockSpec((s, hw), lambda h, i: (0, n_heads + h)),
            pl.BlockSpec((s, hw), lambda h, i: (0, 2 * n_heads + h)),
            pl.BlockSpec((2, tk, hw), lambda h, i: (0, 0, 0)),
            pl.BlockSpec((tk, hw), const2),
            pl.BlockSpec((tq, tq), const2),
            pl.BlockSpec((1, 2, hw), lambda h, i: (h, 0, 0)),
            pl.BlockSpec((1, 1, V7X_LANES), lambda h, i: (h, 0, 0)),
            pl.BlockSpec((1, dh), const2),
            pl.BlockSpec((1, dh), const2),
            pl.BlockSpec((1, dh), const2),
            pl.BlockSpec((1, dh), const2),
            pl.BlockSpec((1, hw), const2),
        ],
        out_specs=pl.BlockSpec((tq, hw), lambda h, i: (i, h)),
        out_shape=jax.ShapeDtypeStruct((s, n_heads * hw), BF16),
        scratch_shapes=[
            pltpu.VMEM((2, s, hw), BF16),
            pltpu.VMEM((s, 2 * hw), BF16),
            pltpu.VMEM((2, tq, V7X_LANES), F32),
            pltpu.VMEM((2, tq, 2 * hw), F32),
            pltpu.VMEM((2, 2, tq, tk), F32),
        ],
        compiler_params=_cparams(("arbitrary", "arbitrary"), vmem),
        name="diff_attention",
    )(qkv, qkv, qkv, pos, ones, tri_bias, qc, shift, vec(lq1), vec(lk1), vec(lq2), vec(lk2), vec(subln_w))


def _rg_lru_kernel(rx_ref, rg_ref, cw_ref, cb_ref, wa_ref, ba_ref, wx_ref, bx_ref, lam_ref, o_ref,
                   ext_ref, a_ref, b_ref, tail_ref, h_ref, *, t_rows, cw, blk):
    t = pl.program_id(1)
    halo = V7X_SUBLANES

    @pl.when(t == 0)
    def _():
        tail_ref[...] = jnp.zeros(tail_ref.shape, F32)
        h_ref[...] = jnp.zeros(h_ref.shape, F32)

    rx = rx_ref[...]
    ext_ref[0:halo, :] = tail_ref[...]
    ext_ref[halo:, :] = rx
    tail_ref[...] = rx[t_rows - halo:, :]
    taps = cw_ref[...]
    n_taps = taps.shape[0]
    xr = cb_ref[...]
    for j in range(n_taps):
        lag = n_taps - 1 - j
        xs = rx if lag == 0 else ext_ref[halo - lag:halo - lag + t_rows, :]
        xr = xr + xs * taps[j:j + 1, :]

    ra, ia = [], []
    for g in range(cw // blk):
        xb = xr[:, g * blk:(g + 1) * blk].astype(BF16)
        ra.append(jnp.dot(xb, wa_ref[g], preferred_element_type=F32))
        ia.append(jnp.dot(xb, wx_ref[g], preferred_element_type=F32))
    r = jax.nn.sigmoid(jnp.concatenate(ra, axis=-1) + ba_ref[...])
    ig = jax.nn.sigmoid(jnp.concatenate(ia, axis=-1) + bx_ref[...])
    neg_lam = -lam_ref[...]
    softplus = jnp.maximum(neg_lam, 0.0) + jnp.log1p(jnp.exp(-jnp.abs(neg_lam)))
    log_a = (-RG_C) * r * softplus
    th = jnp.tanh(log_a)
    a_ref[...] = jnp.exp(log_a)
    b_ref[...] = jnp.sqrt(-2.0 * th / (1.0 - th)) * (ig * xr)

    row = lax.broadcasted_iota(jnp.int32, (V7X_SUBLANES, cw), 0)

    def step(n, h):
        rows = pl.ds(pl.multiple_of(n * V7X_SUBLANES, V7X_SUBLANES), V7X_SUBLANES)
        a = a_ref[rows, :]
        b = b_ref[rows, :]
        d = 1
        while d < V7X_SUBLANES:
            keep = row >= d
            a_s = jnp.where(keep, pltpu.roll(a, d, 0), 1.0)
            b_s = jnp.where(keep, pltpu.roll(b, d, 0), 0.0)
            b = a * b_s + b
            a = a * a_s
            d *= 2
        hh = a * h + b
        b_ref[rows, :] = hh
        return hh[V7X_SUBLANES - 1:, :]

    h_last = lax.fori_loop(0, t_rows // V7X_SUBLANES, step, h_ref[0:1, :], unroll=4)
    h_ref[...] = jnp.broadcast_to(h_last, h_ref.shape)
    o_ref[...] = (b_ref[...] * jax.nn.gelu(rg_ref[...])).astype(o_ref.dtype)


def _rg_lru(rest, conv_w, conv_b, w_a, b_a, w_x, b_x, lru_lambda, d_rnn):
    s = rest.shape[0]
    n_blk, blk, _ = w_a.shape
    cw = _tile(d_rnn, 512, blk)
    t_rows = _tile(s, 1024, 16)
    n_cb = d_rnn // cw
    row = lambda a: a.reshape(1, -1).astype(F32)
    kern = functools.partial(_rg_lru_kernel, t_rows=t_rows, cw=cw, blk=blk)
    vec_spec = pl.BlockSpec((1, cw), lambda n, t: (0, n))
    gate_spec = pl.BlockSpec((cw // blk, blk, blk), lambda n, t: (n, 0, 0))
    vmem = 2 * (2 * t_rows * cw * 4 + t_rows * cw * 2) + 12 * t_rows * cw * 4 + 4 * MIB
    return pl.pallas_call(
        kern,
        grid=(n_cb, s // t_rows),
        in_specs=[
            pl.BlockSpec((t_rows, cw), lambda n, t: (t, n)),
            pl.BlockSpec((t_rows, cw), lambda n, t: (t, n_cb + n)),
            pl.BlockSpec((conv_w.shape[0], cw), lambda n, t: (0, n)),
            vec_spec, gate_spec, vec_spec, gate_spec, vec_spec, vec_spec,
        ],
        out_specs=pl.BlockSpec((t_rows, cw), lambda n, t: (t, n)),
        out_shape=jax.ShapeDtypeStruct((s, d_rnn), BF16),
        scratch_shapes=[
            pltpu.VMEM((t_rows + V7X_SUBLANES, cw), F32),
            pltpu.VMEM((t_rows, cw), F32),
            pltpu.VMEM((t_rows, cw), F32),
            pltpu.VMEM((V7X_SUBLANES, cw), F32),
            pltpu.VMEM((V7X_SUBLANES, cw), F32),
        ],
        compiler_params=_cparams(("parallel", "arbitrary"), vmem),
        name="rg_lru",
    )(rest, rest, conv_w.astype(F32), row(conv_b), w_a.astype(BF16), row(b_a), w_x.astype(BF16), row(b_x),
      row(lru_lambda))


def _gated_merge_kernel(ya_ref, yr_ref, wa_ref, wr_ref, ga_ref, gb_ref, o_ref, wab_ref, wrb_ref):
    _cast_weight_once(wa_ref, wab_ref)
    _cast_weight_once(wr_ref, wrb_ref)
    pa = jnp.dot(ya_ref[...], wab_ref[...], preferred_element_type=F32)
    pr = jnp.dot(yr_ref[...], wrb_ref[...], preferred_element_type=F32)
    o_ref[...] = (jax.nn.sigmoid(ga_ref[...]) * pa + jax.nn.sigmoid(gb_ref[...]) * pr).astype(o_ref.dtype)


def _gated_merge(y_att, y_rec, w_pa, w_pr, rest, ga_col, gb_col):
    m, ka = y_att.shape
    _, kr = y_rec.shape
    n = w_pa.shape[1]
    tm = _tile(m, 1024, 16)
    tn = _tile(math.gcd(math.gcd(n, ga_col), gb_col), 512, V7X_LANES)
    vmem = (2 * (tm * (ka + kr) * 2 + (ka + kr) * tn * 4 + 2 * tm * tn * 4 + tm * tn * 2) + (ka + kr) * tn * 2
            + 4 * tm * tn * 4 + 2 * MIB)
    return pl.pallas_call(
        _gated_merge_kernel,
        grid=(n // tn, m // tm),
        in_specs=[
            pl.BlockSpec((tm, ka), lambda j, i: (i, 0)),
            pl.BlockSpec((tm, kr), lambda j, i: (i, 0)),
            pl.BlockSpec((ka, tn), lambda j, i: (0, j)),
            pl.BlockSpec((kr, tn), lambda j, i: (0, j)),
            pl.BlockSpec((tm, tn), lambda j, i: (i, ga_col // tn + j)),
            pl.BlockSpec((tm, tn), lambda j, i: (i, gb_col // tn + j)),
        ],
        out_specs=pl.BlockSpec((tm, tn), lambda j, i: (i, j)),
        out_shape=jax.ShapeDtypeStruct((m, n), BF16),
        scratch_shapes=[pltpu.VMEM((ka, tn), BF16), pltpu.VMEM((kr, tn), BF16)],
        compiler_params=_cparams(("arbitrary", "arbitrary"), vmem),
        name="gated_merge",
    )(y_att, y_rec, w_pa, w_pr, rest, rest)


def _matmul_residual_kernel(a_ref, w_ref, x_ref, g_ref, o_ref, wb_ref):
    _cast_weight_once(w_ref, wb_ref)
    y = jnp.dot(a_ref[...], wb_ref[...], preferred_element_type=F32)
    o_ref[...] = x_ref[...] + g_ref[...] * y


def _matmul_residual(a, w, x, mod, gate_idx):
    m, k = a.shape
    n = w.shape[1]
    tm = _tile(m, 1024, 16)
    tn = _tile(n, 512, V7X_LANES)
    vmem = 2 * (tm * k * 2 + k * tn * 4 + 2 * tm * tn * 4) + k * tn * 2 + tm * tn * 4 + 2 * MIB
    return pl.pallas_call(
        _matmul_residual_kernel,
        grid=(n // tn, m // tm),
        in_specs=[
            pl.BlockSpec((tm, k), lambda j, i: (i, 0)),
            pl.BlockSpec((k, tn), lambda j, i: (0, j)),
            pl.BlockSpec((tm, tn), lambda j, i: (i, j)),
            pl.BlockSpec((1, tn), lambda j, i: (0, gate_idx * (n // tn) + j)),
        ],
        out_specs=pl.BlockSpec((tm, tn), lambda j, i: (i, j)),
        out_shape=jax.ShapeDtypeStruct((m, n), F32),
        scratch_shapes=[pltpu.VMEM((k, tn), BF16)],
        compiler_params=_cparams(("arbitrary", "arbitrary"), vmem),
        name="matmul_residual",
    )(a, w, x, mod)


def _conv_ffn_kernel(h_ref, wg_ref, wu_ref, cw_ref, cb_ref, wd_ref, x_ref, g_ref, nw_ref, o_ref,
                     gate_ref, up_ref, act_ref, carry_ref, *, tm, dn, rn, rc, nf, final_norm):
    mi = pl.program_id(0)
    f = pl.program_id(1)
    halo = V7X_SUBLANES
    d = o_ref.shape[1]

    def gate_up(slot):
        @pl.when(mi == 0)
        def _():
            carry_ref[f] = jnp.zeros(carry_ref.shape[1:], F32)

        h = h_ref[...]
        gate = jnp.dot(h, wg_ref[...], preferred_element_type=F32)
        up_ref[slot] = jnp.dot(h, wu_ref[...], preferred_element_type=F32)
        gate_ref[slot, 0:halo, :] = carry_ref[f]
        gate_ref[slot, halo:, :] = gate
        carry_ref[f] = gate[tm - halo:, :]

    def activation_rows(slot, r):
        r0 = r * rc
        taps = cw_ref[f - 1]
        n_taps = taps.shape[0]
        gc = cb_ref[f - 1]
        for j in range(n_taps):
            start = halo - (n_taps - 1 - j) + r0
            gc = gc + gate_ref[slot, start:start + rc, :] * taps[j:j + 1, :]
        act_ref[slot, r0:r0 + rc, :] = (jax.nn.gelu(gc) * up_ref[slot, r0:r0 + rc, :]).astype(BF16)

    def down_cols(slot, c):
        cols = slice(c * dn, (c + 1) * dn)
        o_ref[:, cols] += jnp.dot(act_ref[slot], wd_ref[:, cols], preferred_element_type=F32)

    def step(gu_slot, act_slot, down_slot):
        n_c = d // dn
        n_r = tm // rc
        if gu_slot is not None:
            gate_up(gu_slot)
        for k in range(max(n_c, n_r)):
            if act_slot is not None and k < n_r:
                activation_rows(act_slot, k)
            if down_slot is not None and k < n_c:
                down_cols(down_slot, k)

    @pl.when(f == 0)
    def _():
        o_ref[...] = jnp.zeros(o_ref.shape, F32)
        step(0, None, None)

    @pl.when(f == 1)
    def _():
        step(1, 0, None)

    @pl.when(jnp.logical_and(jnp.logical_and(f >= 2, f < nf), f % 2 == 0))
    def _():
        step(0, 1, 0)

    @pl.when(jnp.logical_and(jnp.logical_and(f >= 2, f < nf), f % 2 == 1))
    def _():
        step(1, 0, 1)

    @pl.when(f == nf)
    def _():
        step(None, (nf - 1) % 2, nf % 2)

    @pl.when(f == nf + 1)
    def _():
        step(None, None, (nf - 1) % 2)
        for r in range(tm // rn):
            rows = slice(r * rn, (r + 1) * rn)
            x2 = x_ref[rows, :] + g_ref[...] * o_ref[rows, :]
            if final_norm:
                x2 = x2 * _rms_scale(x2) * nw_ref[...]
            o_ref[rows, :] = x2


def _conv_ffn(h, w_up, conv_w, conv_b, w_down, x, mod, gate_idx, norm_w, final_norm):
    s, d = h.shape
    d_ff = w_down.shape[0]
    tm = _tile(s, 512, 16)
    tf = _tile(d_ff, 256, V7X_LANES)
    nf = d_ff // tf
    assert nf >= 2
    dn = _tile(d, 512, V7X_LANES)
    rn = _tile(tm, 128, V7X_SUBLANES)
    rc = _tile(tm, 64, 16)
    kern = functools.partial(_conv_ffn_kernel, tm=tm, dn=dn, rn=rn, rc=rc, nf=nf, final_norm=final_norm)
    vmem = (2 * (tm * d * 2 + 2 * d * tf * 2 + tf * d * 2 + tm * d * 4) + tm * d * 4
            + 12 * tm * tf * 4 + 4 * rn * d * 4 + tm * dn * 4 + 2 * MIB)
    chunk = lambda f, lag: jnp.clip(f - lag, 0, nf - 1)
    n_taps = conv_w.shape[0]
    taps_by_chunk = conv_w.astype(F32).reshape(n_taps, nf, tf).transpose(1, 0, 2)
    bias_by_chunk = conv_b.astype(F32).reshape(nf, 1, tf)
    return pl.pallas_call(
        kern,
        grid=(s // tm, nf + 2),
        in_specs=[
            pl.BlockSpec((tm, d), lambda i, f: (i, 0)),
            pl.BlockSpec((d, tf), lambda i, f: (0, chunk(f, 0))),
            pl.BlockSpec((d, tf), lambda i, f: (0, nf + chunk(f, 0))),
            pl.BlockSpec((nf, n_taps, tf), lambda i, f: (0, 0, 0)),
            pl.BlockSpec((nf, 1, tf), lambda i, f: (0, 0, 0)),
            pl.BlockSpec((tf, d), lambda i, f: (chunk(f, 2), 0)),
            pl.BlockSpec((tm, d), lambda i, f: (i, 0), pipeline_mode=pl.Buffered(1)),
            pl.BlockSpec((1, d), lambda i, f: (0, gate_idx)),
            pl.BlockSpec((1, d), lambda i, f: (0, 0)),
        ],
        out_specs=pl.BlockSpec((tm, d), lambda i, f: (i, 0)),
        out_shape=jax.ShapeDtypeStruct((s, d), F32),
        scratch_shapes=[
            pltpu.VMEM((2, tm + V7X_SUBLANES, tf), F32),
            pltpu.VMEM((2, tm, tf), F32),
            pltpu.VMEM((2, tm, tf), BF16),
            pltpu.VMEM((nf, V7X_SUBLANES, tf), F32),
        ],
        compiler_params=_cparams(("arbitrary", "arbitrary"), vmem),
        name="conv_ffn",
    )(h, w_up, w_up, taps_by_chunk, bias_by_chunk, w_down, x, mod, norm_w.reshape(1, -1).astype(F32))


def kernel(x, c, w_ada, b_ada, norm1_w, w_in, lambda_q1, lambda_k1, lambda_q2, lambda_k2, subln_w, conv_w, conv_b, w_rg_a, b_rg_a, w_rg_x, b_rg_x, lru_lambda, w_proj_attn, w_proj_rec, w_out, norm2_w, w_ffn_up, ffn_conv_w, ffn_conv_b, w_ffn_down, norm_f_w):
    batch, seq, d = x.shape
    depth = w_ada.shape[0]
    dh = lambda_q1.shape[-1]
    attn_w = w_proj_attn.shape[1]
    n_heads = attn_w // (2 * dh)
    d_rnn = conv_w.shape[-1]
    assert w_in.shape[-1] == 3 * attn_w + 2 * d_rnn + 2 * d
    row = lambda a: a.reshape(1, -1).astype(F32)

    outs = []
    for b in range(batch):
        xb = x[b]
        cb = c[b:b + 1]
        for l in range(depth):
            lambda_init = 0.8 - 0.6 * math.exp(-0.3 * l)
            mod = _adaln_mod(cb, w_ada[l], row(b_ada[l]))
            h1 = _norm_modulate(xb, row(norm1_w[l]), mod, 0, 1)
            q_scale = jnp.concatenate([jnp.full((1, attn_w), dh ** -0.5 * LOG2E, F32),
                                       jnp.ones((1, 2 * attn_w), F32)], axis=1)
            qkv = _matmul(h1, w_in[l], 0, 3 * attn_w, BF16, q_scale)
            rest = _matmul(h1, w_in[l], 3 * attn_w, 2 * d_rnn + 2 * d, F32)
            y_att = _diff_attention(qkv, lambda_q1[l], lambda_k1[l], lambda_q2[l], lambda_k2[l], subln_w[l],
                                    lambda_init, n_heads, dh)
            y_rec = _rg_lru(rest, conv_w[l], conv_b[l], w_rg_a[l], b_rg_a[l], w_rg_x[l], b_rg_x[l],
                            lru_lambda[l], d_rnn)
            mixed = _gated_merge(y_att, y_rec, w_proj_attn[l], w_proj_rec[l], rest, 2 * d_rnn, 2 * d_rnn + d)
            x1 = _matmul_residual(mixed, w_out[l], xb, mod, 2)
            h2 = _norm_modulate(x1, row(norm2_w[l]), mod, 3, 4)
            last = l == depth - 1
            xb = _conv_ffn(h2, w_ffn_up[l].astype(BF16), ffn_conv_w[l], ffn_conv_b[l], w_ffn_down[l].astype(BF16),
                           x1, mod, 5, norm_f_w, final_norm=last)
        outs.append(xb)
    return outs[0][None] if batch == 1 else jnp.stack(outs, axis=0)
```

```python
import functools
import math

import jax
import jax.numpy as jnp
from jax import lax
from jax.experimental import pallas as pl
from jax.experimental.pallas import tpu as pltpu

F32 = jnp.float32
BF16 = jnp.bfloat16
EPS = 1e-6
RG_C = 8.0
NEG_BIG = -1e30

V7X_LANES = 128
V7X_SUBLANES = 8
BF16_SUBLANES = 16
V7X_VMEM_LIMIT_BYTES = 60000 * 1024
MIB = 1024 * 1024


def _cparams(semantics, vmem_bytes):
    return pltpu.CompilerParams(
        dimension_semantics=semantics,
        vmem_limit_bytes=int(min(vmem_bytes, V7X_VMEM_LIMIT_BYTES)),
    )


def _tile(n, want, quantum):
    t = min(n, want) // quantum * quantum
    while t > quantum and n % t:
        t -= quantum
    assert t >= quantum and n % t == 0, (n, want, quantum)
    return t


def _rms_scale(x):
    return lax.rsqrt(jnp.mean(x * x, axis=-1, keepdims=True) + EPS)


def _adaln_mod_kernel(c_ref, w_ref, b_ref, o_ref):
    c = c_ref[...]
    a = c * jax.nn.sigmoid(c)
    a8 = jnp.broadcast_to(a, (V7X_SUBLANES, a.shape[1])).astype(BF16)
    acc = jnp.dot(a8, w_ref[...].astype(BF16), preferred_element_type=F32)
    o_ref[...] = acc[0:1, :] + b_ref[...]


def _adaln_mod(c, w, b):
    d, n = w.shape
    tn = _tile(n, 1024, V7X_LANES)
    return pl.pallas_call(
        _adaln_mod_kernel,
        grid=(n // tn,),
        in_specs=[
            pl.BlockSpec((1, d), lambda j: (0, 0)),
            pl.BlockSpec((d, tn), lambda j: (0, j)),
            pl.BlockSpec((1, tn), lambda j: (0, j)),
        ],
        out_specs=pl.BlockSpec((1, tn), lambda j: (0, j)),
        out_shape=jax.ShapeDtypeStruct((1, n), F32),
        compiler_params=_cparams(("arbitrary",), 2 * d * tn * 4 + d * tn * 2 + 4 * MIB),
        name="adaln_mod",
    )(c, w, b)


def _norm_modulate_kernel(x_ref, w_ref, scale_ref, shift_ref, o_ref):
    x = x_ref[...]
    y = x * _rms_scale(x) * w_ref[...]
    o_ref[...] = (y * (1.0 + scale_ref[...]) + shift_ref[...]).astype(o_ref.dtype)


def _norm_modulate(x, w, mod, shift_idx, scale_idx):
    s, d = x.shape
    tm = _tile(s, 256, BF16_SUBLANES)
    return pl.pallas_call(
        _norm_modulate_kernel,
        grid=(s // tm,),
        in_specs=[
            pl.BlockSpec((tm, d), lambda i: (i, 0)),
            pl.BlockSpec((1, d), lambda i: (0, 0)),
            pl.BlockSpec((1, d), lambda i: (0, scale_idx)),
            pl.BlockSpec((1, d), lambda i: (0, shift_idx)),
        ],
        out_specs=pl.BlockSpec((tm, d), lambda i: (i, 0)),
        out_shape=jax.ShapeDtypeStruct((s, d), BF16),
        compiler_params=_cparams(("parallel",), 2 * tm * d * 6 + 3 * tm * d * 4 + 2 * MIB),
        name="norm_modulate",
    )(x, w, mod, mod)


def _cast_weight_once(w_ref, wb_ref):
    @pl.when(pl.program_id(1) == 0)
    def _():
        wb_ref[...] = w_ref[...].astype(BF16)


def _matmul_kernel(x_ref, w_ref, o_ref, wb_ref):
    _cast_weight_once(w_ref, wb_ref)
    o_ref[...] = jnp.dot(x_ref[...], wb_ref[...], preferred_element_type=F32).astype(o_ref.dtype)


def _matmul_colscale_kernel(x_ref, w_ref, s_ref, o_ref, wb_ref):
    _cast_weight_once(w_ref, wb_ref)
    y = jnp.dot(x_ref[...], wb_ref[...], preferred_element_type=F32)
    o_ref[...] = (y * s_ref[...]).astype(o_ref.dtype)


def _matmul(x, w, col0, n, out_dtype, col_scale=None):
    m, k = x.shape
    tm = _tile(m, 1024, BF16_SUBLANES)
    tn = _tile(math.gcd(n, col0) if col0 else n, 512, V7X_LANES)
    ob = jnp.dtype(out_dtype).itemsize
    vmem = 2 * (tm * k * 2 + k * tn * 4 + tm * tn * ob) + k * tn * 2 + 2 * tm * tn * 4 + 2 * MIB
    in_specs = [
        pl.BlockSpec((tm, k), lambda j, i: (i, 0)),
        pl.BlockSpec((k, tn), lambda j, i: (0, col0 // tn + j)),
    ]
    args = (x, w)
    if col_scale is not None:
        in_specs.append(pl.BlockSpec((1, tn), lambda j, i: (0, j)))
        args = (x, w, col_scale)
    return pl.pallas_call(
        _matmul_kernel if col_scale is None else _matmul_colscale_kernel,
        grid=(n // tn, m // tm),
        in_specs=in_specs,
        out_specs=pl.BlockSpec((tm, tn), lambda j, i: (i, j)),
        out_shape=jax.ShapeDtypeStruct((m, n), out_dtype),
        scratch_shapes=[pltpu.VMEM((k, tn), BF16)],
        compiler_params=_cparams(("arbitrary", "arbitrary"), vmem),
        name="matmul",
    )(*args)


LOG2E = math.log2(math.e)
POS_RADIX = 32
N_BIAS_COLS = 6


def _split3_bf16(x):
    hi = x.astype(BF16).astype(F32)
    mid = (x - hi).astype(BF16).astype(F32)
    lo = (x - hi - mid).astype(BF16).astype(F32)
    return hi, mid, lo


def _diff_attn_kernel(q_ref, k_ref, v_ref, pos_ref, ones_ref, tri_ref, qc_ref, sh_ref, lq1_ref, lk1_ref, lq2_ref, lk2_ref,
                      sw_ref, o_ref, ka_ref, va_ref, m_ref, acc_ref, z_ref, *, tq, tk, dh, lambda_init):
    i = pl.program_id(1)
    hw = 2 * dh
    s_len = k_ref.shape[0]

    @pl.when(i == 0)
    def _():
        lane_k = lax.broadcasted_iota(jnp.int32, (tk, hw), 1)

        def prep(j, carry):
            rows = pl.ds(pl.multiple_of(j * tk, tk), tk)
            kj = k_ref[rows, :]
            ka_ref[0, rows, :] = jnp.where(lane_k < dh, kj, pos_ref[0])
            ka_ref[1, rows, :] = jnp.where(lane_k >= dh, kj, pos_ref[1])
            va_ref[rows, 0:hw] = v_ref[rows, :]
            va_ref[rows, hw:2 * hw] = ones_ref[...]
            return carry

        lax.fori_loop(0, s_len // tk, prep, 0)

    lane_q = lax.broadcasted_iota(jnp.int32, (tq, hw), 1)
    qc = qc_ref[0].astype(BF16)

    def score_operands(q):
        return (jnp.where(lane_q < dh, q, qc[0:1, :]), jnp.where(lane_q >= dh, q, qc[1:2, :]))

    q_maps = score_operands(q_ref[...])
    frame_shift = sh_ref[0][:, 0:1]

    m_ref[...] = jnp.full(m_ref.shape, NEG_BIG, F32)
    acc_ref[...] = jnp.zeros(acc_ref.shape, F32)

    def scores(maps, j, c):
        rows = pl.ds(pl.multiple_of(j * tk, tk), tk)
        return lax.dot_general(maps[c], ka_ref[c, rows, :], (((1,), (1,)), ((), ())),
                               preferred_element_type=F32)

    def lanes(x, width):
        return jnp.concatenate([x] * (width // V7X_LANES), axis=-1)

    def update(j, c, z):
        width = z.shape[1]
        rows = pl.ds(pl.multiple_of(j * tk, tk), width)
        m_prev = m_ref[c] - frame_shift
        m_new = jnp.maximum(m_prev, jnp.max(z, axis=-1, keepdims=True))
        alpha = jnp.exp2(m_prev - m_new)
        p = jnp.exp2(z - lanes(m_new, width)).astype(BF16)
        acc_ref[c] = lanes(alpha, 2 * hw) * acc_ref[c] + jnp.dot(p, va_ref[rows, :], preferred_element_type=F32)
        m_ref[c] = m_new

    n_full = (i * tq) // tk

    def step(j, cur):
        for c in range(2):
            z_ref[1 - cur, c] = scores(q_maps, j + 1, c)
        for c in range(2):
            update(j, c, z_ref[cur, c])

    def diagonal_step(cur):
        for pos in range(tk // tq):
            @pl.when(i % (tk // tq) == pos)
            def _():
                lo, width = pos * tq, (pos + 1) * tq
                for c in range(2):
                    diag = z_ref[cur, c, :, lo:width] + tri_ref[...]
                    z = diag if pos == 0 else jnp.concatenate([z_ref[cur, c, :, 0:lo], diag], axis=-1)
                    update(n_full, c, z)

    for c in range(2):
        z_ref[0, c] = scores(q_maps, 0, c)

    def chunk_pair(t, carry):
        step(2 * t, 0)
        step(2 * t + 1, 1)
        return carry

    lax.fori_loop(0, n_full // 2, chunk_pair, 0)

    @pl.when(n_full % 2 == 1)
    def _():
        step(n_full - 1, 0)
        diagonal_step(1)

    @pl.when(n_full % 2 == 0)
    def _():
        diagonal_step(0)

    a0 = acc_ref[0]
    a1 = acc_ref[1]
    o1 = a0[:, 0:hw] / a0[:, hw:hw + 1]
    o2 = a1[:, 0:hw] / a1[:, hw:hw + 1]
    lam = (jnp.exp(jnp.sum(lq1_ref[...] * lk1_ref[...], axis=-1, keepdims=True))
           - jnp.exp(jnp.sum(lq2_ref[...] * lk2_ref[...], axis=-1, keepdims=True)) + lambda_init)
    o = o1 - lam * o2
    y = o * _rms_scale(o) * sw_ref[...]
    o_ref[...] = (y * (1.0 - lambda_init)).astype(o_ref.dtype)


def _diff_attention(qkv, lq1, lk1, lq2, lk2, subln_w, lambda_init, n_heads, dh):
    s = qkv.shape[0]
    hw = 2 * dh
    tk = _tile(s, 1024, V7X_LANES)
    tq = _tile(tk, 512, BF16_SUBLANES)
    nb = N_BIAS_COLS
    assert dh >= nb and tk // POS_RADIX <= 256
    slopes2 = (2.0 ** (-8.0 * jnp.arange(1, n_heads + 1, dtype=F32) / n_heads)) * LOG2E
    pieces = jnp.stack(_split3_bf16(slopes2), axis=-1)
    q_cols = jnp.concatenate([pieces * POS_RADIX, pieces], axis=-1)
    qc = jnp.zeros((n_heads, 2, hw), F32).at[:, 0, dh:dh + nb].set(q_cols).at[:, 1, 0:nb].set(q_cols)
    jj = jnp.arange(tk, dtype=jnp.int32)
    k_cols = jnp.stack([jj // POS_RADIX] * 3 + [jj % POS_RADIX] * 3, axis=-1).astype(F32)
    pos = jnp.zeros((2, tk, hw), F32).at[0, :, dh:dh + nb].set(k_cols).at[1, :, 0:nb].set(k_cols).astype(BF16)
    ones = jnp.zeros((tk, hw), F32).at[:, 0].set(1.0).astype(BF16)
    shift = jnp.broadcast_to((slopes2 * tk)[:, None, None], (n_heads, 1, V7X_LANES))
    rel = jnp.arange(tq, dtype=jnp.int32)[None, :] - jnp.arange(tq, dtype=jnp.int32)[:, None]
    tri_bias = jnp.where(rel <= 0, 0.0, NEG_BIG).astype(F32)
    kern = functools.partial(_diff_attn_kernel, tq=tq, tk=tk, dh=dh, lambda_init=lambda_init)
    vec = lambda a: a.reshape(1, -1).astype(F32)
    const2 = lambda h, i: (0, 0)
    vmem = (2 * (2 * s * hw * 2 + 2 * tq * hw * 2 + 3 * tk * hw * 2) + 4 * s * hw * 2
            + 8 * tq * tk * 4 + 2 * tq * (2 * hw + V7X_LANES) * 4 + 4 * MIB)
    return pl.pallas_call(
        kern,
        grid=(n_heads, s // tq),
        in_specs=[
            pl.BlockSpec((tq, hw), lambda h, i: (i, h)),
            pl.BlockSpec((s, hw), lambda h, i: (0, n_heads + h)),
            pl.BlockSpec((s, hw), lambda h, i: (0, 2 * n_heads + h)),
            pl.BlockSpec((2, tk, hw), lambda h, i: (0, 0, 0)),
            pl.BlockSpec((tk, hw), const2),
            pl.BlockSpec((tq, tq), const2),
            pl.BlockSpec((1, 2, hw), lambda h, i: (h, 0, 0)),
            pl.BlockSpec((1, 1, V7X_LANES), lambda h, i: (h, 0, 0)),
            pl.BlockSpec((1, dh), const2),
            pl.BlockSpec((1, dh), const2),
            pl.BlockSpec((1, dh), const2),
            pl.BlockSpec((1, dh), const2),
            pl.BlockSpec((1, hw), const2),
        ],
        out_specs=pl.BlockSpec((tq, hw), lambda h, i: (i, h)),
        out_shape=jax.ShapeDtypeStruct((s, n_heads * hw), BF16),
        scratch_shapes=[
            pltpu.VMEM((2, s, hw), BF16),
            pltpu.VMEM((s, 2 * hw), BF16),
            pltpu.VMEM((2, tq, V7X_LANES), F32),
            pltpu.VMEM((2, tq, 2 * hw), F32),
            pltpu.VMEM((2, 2, tq, tk), F32),
        ],
        compiler_params=_cparams(("arbitrary", "arbitrary"), vmem),
        name="diff_attention",
    )(qkv, qkv, qkv, pos, ones, tri_bias, qc, shift, vec(lq1), vec(lk1), vec(lq2), vec(lk2), vec(subln_w))


def _rg_lru_kernel(rx_ref, rg_ref, cw_ref, cb_ref, wa_ref, ba_ref, wx_ref, bx_ref, lam_ref, o_ref,
                   ext_ref, a_ref, b_ref, tail_ref, h_ref, *, t_rows, cw, blk):
    t = pl.program_id(1)
    halo = V7X_SUBLANES

    @pl.when(t == 0)
    def _():
        tail_ref[...] = jnp.zeros(tail_ref.shape, F32)
        h_ref[...] = jnp.zeros(h_ref.shape, F32)

    rx = rx_ref[...]
    ext_ref[0:halo, :] = tail_ref[...]
    ext_ref[halo:, :] = rx
    tail_ref[...] = rx[t_rows - halo:, :]
    ext = ext_ref[...]
    taps = cw_ref[...]
    n_taps = taps.shape[0]
    xr = cb_ref[...]
    for j in range(n_taps):
        lag = n_taps - 1 - j
        xs = rx if lag == 0 else pltpu.roll(ext, lag, 0)[halo:, :]
        xr = xr + xs * taps[j:j + 1, :]

    ra, ia = [], []
    for g in range(cw // blk):
        xb = xr[:, g * blk:(g + 1) * blk].astype(BF16)
        ra.append(jnp.dot(xb, wa_ref[g], preferred_element_type=F32))
        ia.append(jnp.dot(xb, wx_ref[g], preferred_element_type=F32))
    r = jax.nn.sigmoid(jnp.concatenate(ra, axis=-1) + ba_ref[...])
    ig = jax.nn.sigmoid(jnp.concatenate(ia, axis=-1) + bx_ref[...])
    neg_lam = -lam_ref[...]
    softplus = jnp.maximum(neg_lam, 0.0) + jnp.log1p(jnp.exp(-jnp.abs(neg_lam)))
    log_a = (-RG_C) * r * softplus
    th = jnp.tanh(log_a)
    a_ref[...] = jnp.exp(log_a)
    u = -2.0 * th / (1.0 - th)
    b_ref[...] = jnp.where(u > 0.0, u * lax.rsqrt(u), 0.0) * (ig * xr)

    row = lax.broadcasted_iota(jnp.int32, (V7X_SUBLANES, cw), 0)

    def step(n, h):
        rows = pl.ds(pl.multiple_of(n * V7X_SUBLANES, V7X_SUBLANES), V7X_SUBLANES)
        a = a_ref[rows, :]
        b = b_ref[rows, :]
        d = 1
        while d < V7X_SUBLANES:
            keep = row >= d
            a_s = jnp.where(keep, pltpu.roll(a, d, 0), 1.0)
            b_s = jnp.where(keep, pltpu.roll(b, d, 0), 0.0)
            b = a * b_s + b
            a = a * a_s
            d *= 2
        hh = a * h + b
        b_ref[rows, :] = hh
        return hh[V7X_SUBLANES - 1:, :]

    h_last = lax.fori_loop(0, t_rows // V7X_SUBLANES, step, h_ref[0:1, :], unroll=4)
    h_ref[...] = jnp.broadcast_to(h_last, h_ref.shape)
    o_ref[...] = (b_ref[...] * jax.nn.gelu(rg_ref[...])).astype(o_ref.dtype)


def _rg_lru(rest, conv_w, conv_b, w_a, b_a, w_x, b_x, lru_lambda, d_rnn):
    s = rest.shape[0]
    n_blk, blk, _ = w_a.shape
    cw = _tile(d_rnn, 512, blk)
    t_rows = _tile(s, 1024, BF16_SUBLANES)
    n_cb = d_rnn // cw
    row = lambda a: a.reshape(1, -1).astype(F32)
    kern = functools.partial(_rg_lru_kernel, t_rows=t_rows, cw=cw, blk=blk)
    vec_spec = pl.BlockSpec((1, cw), lambda n, t: (0, n))
    gate_spec = pl.BlockSpec((cw // blk, blk, blk), lambda n, t: (n, 0, 0))
    vmem = 2 * (2 * t_rows * cw * 4 + t_rows * cw * 2) + 12 * t_rows * cw * 4 + 4 * MIB
    return pl.pallas_call(
        kern,
        grid=(n_cb, s // t_rows),
        in_specs=[
            pl.BlockSpec((t_rows, cw), lambda n, t: (t, n)),
            pl.BlockSpec((t_rows, cw), lambda n, t: (t, n_cb + n)),
            pl.BlockSpec((conv_w.shape[0], cw), lambda n, t: (0, n)),
            vec_spec, gate_spec, vec_spec, gate_spec, vec_spec, vec_spec,
        ],
        out_specs=pl.BlockSpec((t_rows, cw), lambda n, t: (t, n)),
        out_shape=jax.ShapeDtypeStruct((s, d_rnn), BF16),
        scratch_shapes=[
            pltpu.VMEM((t_rows + V7X_SUBLANES, cw), F32),
            pltpu.VMEM((t_rows, cw), F32),
            pltpu.VMEM((t_rows, cw), F32),
            pltpu.VMEM((V7X_SUBLANES, cw), F32),
            pltpu.VMEM((V7X_SUBLANES, cw), F32),
        ],
        compiler_params=_cparams(("parallel", "arbitrary"), vmem),
        name="rg_lru",
    )(rest, rest, conv_w.astype(F32), row(conv_b), w_a.astype(BF16), row(b_a), w_x.astype(BF16), row(b_x),
      row(lru_lambda))


def _gated_merge_kernel(ya_ref, yr_ref, wa_ref, wr_ref, ga_ref, gb_ref, o_ref, wab_ref, wrb_ref):
    _cast_weight_once(wa_ref, wab_ref)
    _cast_weight_once(wr_ref, wrb_ref)
    pa = jnp.dot(ya_ref[...], wab_ref[...], preferred_element_type=F32)
    pr = jnp.dot(yr_ref[...], wrb_ref[...], preferred_element_type=F32)
    o_ref[...] = (jax.nn.sigmoid(ga_ref[...]) * pa + jax.nn.sigmoid(gb_ref[...]) * pr).astype(o_ref.dtype)


def _gated_merge(y_att, y_rec, w_pa, w_pr, rest, ga_col, gb_col):
    m, ka = y_att.shape
    _, kr = y_rec.shape
    n = w_pa.shape[1]
    tm = _tile(m, 1024, BF16_SUBLANES)
    tn = _tile(math.gcd(math.gcd(n, ga_col), gb_col), 512, V7X_LANES)
    vmem = (2 * (tm * (ka + kr) * 2 + (ka + kr) * tn * 4 + 2 * tm * tn * 4 + tm * tn * 2) + (ka + kr) * tn * 2
            + 4 * tm * tn * 4 + 2 * MIB)
    return pl.pallas_call(
        _gated_merge_kernel,
        grid=(n // tn, m // tm),
        in_specs=[
            pl.BlockSpec((tm, ka), lambda j, i: (i, 0)),
            pl.BlockSpec((tm, kr), lambda j, i: (i, 0)),
            pl.BlockSpec((ka, tn), lambda j, i: (0, j)),
            pl.BlockSpec((kr, tn), lambda j, i: (0, j)),
            pl.BlockSpec((tm, tn), lambda j, i: (i, ga_col // tn + j)),
            pl.BlockSpec((tm, tn), lambda j, i: (i, gb_col // tn + j)),
        ],
        out_specs=pl.BlockSpec((tm, tn), lambda j, i: (i, j)),
        out_shape=jax.ShapeDtypeStruct((m, n), BF16),
        scratch_shapes=[pltpu.VMEM((ka, tn), BF16), pltpu.VMEM((kr, tn), BF16)],
        compiler_params=_cparams(("arbitrary", "arbitrary"), vmem),
        name="gated_merge",
    )(y_att, y_rec, w_pa, w_pr, rest, rest)


def _matmul_residual_kernel(a_ref, w_ref, x_ref, g_ref, o_ref, wb_ref):
    _cast_weight_once(w_ref, wb_ref)
    y = jnp.dot(a_ref[...], wb_ref[...], preferred_element_type=F32)
    o_ref[...] = x_ref[...] + g_ref[...] * y


def _matmul_residual(a, w, x, mod, gate_idx):
    m, k = a.shape
    n = w.shape[1]
    tm = _tile(m, 1024, BF16_SUBLANES)
    tn = _tile(n, 512, V7X_LANES)
    vmem = 2 * (tm * k * 2 + k * tn * 4 + 2 * tm * tn * 4) + k * tn * 2 + tm * tn * 4 + 2 * MIB
    return pl.pallas_call(
        _matmul_residual_kernel,
        grid=(n // tn, m // tm),
        in_specs=[
            pl.BlockSpec((tm, k), lambda j, i: (i, 0)),
            pl.BlockSpec((k, tn), lambda j, i: (0, j)),
            pl.BlockSpec((tm, tn), lambda j, i: (i, j)),
            pl.BlockSpec((1, tn), lambda j, i: (0, gate_idx * (n // tn) + j)),
        ],
        out_specs=pl.BlockSpec((tm, tn), lambda j, i: (i, j)),
        out_shape=jax.ShapeDtypeStruct((m, n), F32),
        scratch_shapes=[pltpu.VMEM((k, tn), BF16)],
        compiler_params=_cparams(("arbitrary", "arbitrary"), vmem),
        name="matmul_residual",
    )(a, w, x, mod)


def _conv_ffn_kernel(x_ref, n2w_ref, scale_ref, shift_ref, wg_ref, wu_ref, cw_ref, cb_ref, wd_ref, g_ref, nw_ref,
                     o_ref, h_ref, gate_ref, up_ref, act_ref, carry_ref, *, tm, dn, rn, rc, nf, final_norm):
    mi = pl.program_id(0)
    f = pl.program_id(1)
    halo = V7X_SUBLANES
    d = o_ref.shape[1]

    def gate_up(slot):
        @pl.when(mi == 0)
        def _():
            carry_ref[f] = jnp.zeros(carry_ref.shape[1:], F32)

        h = h_ref[...]
        gate = jnp.dot(h, wg_ref[...], preferred_element_type=F32)
        up_ref[slot] = jnp.dot(h, wu_ref[...], preferred_element_type=F32)
        gate_ref[slot, 0:halo, :] = carry_ref[f]
        gate_ref[slot, halo:, :] = gate
        carry_ref[f] = gate[tm - halo:, :]

    def activation_rows(slot, r):
        r0 = r * rc
        taps = cw_ref[f - 1]
        n_taps = taps.shape[0]
        gc = cb_ref[f - 1]
        for j in range(n_taps):
            start = halo - (n_taps - 1 - j) + r0
            gc = gc + gate_ref[slot, start:start + rc, :] * taps[j:j + 1, :]
        act_ref[slot, r0:r0 + rc, :] = (jax.nn.gelu(gc) * up_ref[slot, r0:r0 + rc, :]).astype(BF16)

    def down_cols(slot, c):
        cols = slice(c * dn, (c + 1) * dn)
        o_ref[:, cols] += jnp.dot(act_ref[slot], wd_ref[:, cols], preferred_element_type=F32)

    def step(gu_slot, act_slot, down_slot):
        n_c = d // dn
        n_r = tm // rc
        if gu_slot is not None:
            gate_up(gu_slot)
        for k in range(max(n_c, n_r)):
            if act_slot is not None and k < n_r:
                activation_rows(act_slot, k)
            if down_slot is not None and k < n_c:
                down_cols(down_slot, k)

    @pl.when(f == 0)
    def _():
        gain = n2w_ref[...] * (1.0 + scale_ref[...])
        for r in range(tm // rn):
            rows = slice(r * rn, (r + 1) * rn)
            x = x_ref[rows, :]
            h_ref[rows, :] = (x * _rms_scale(x) * gain + shift_ref[...]).astype(BF16)
        o_ref[...] = jnp.zeros(o_ref.shape, F32)
        step(0, None, None)

    @pl.when(f == 1)
    def _():
        step(1, 0, None)

    @pl.when(jnp.logical_and(jnp.logical_and(f >= 2, f < nf), f % 2 == 0))
    def _():
        step(0, 1, 0)

    @pl.when(jnp.logical_and(jnp.logical_and(f >= 2, f < nf), f % 2 == 1))
    def _():
        step(1, 0, 1)

    @pl.when(f == nf)
    def _():
        step(None, (nf - 1) % 2, nf % 2)

    @pl.when(f == nf + 1)
    def _():
        step(None, None, (nf - 1) % 2)
        for r in range(tm // rn):
            rows = slice(r * rn, (r + 1) * rn)
            x2 = x_ref[rows, :] + g_ref[...] * o_ref[rows, :]
            if final_norm:
                x2 = x2 * _rms_scale(x2) * nw_ref[...]
            o_ref[rows, :] = x2


def _conv_ffn(x, norm2_w, mod, shift_idx, scale_idx, gate_idx, w_up, conv_w, conv_b, w_down, norm_w, final_norm):
    s, d = x.shape
    d_ff = w_down.shape[0]
    tm = _tile(s, 512, BF16_SUBLANES)
    tf = _tile(d_ff, 256, V7X_LANES)
    nf = d_ff // tf
    assert nf >= 2
    dn = _tile(d, 512, V7X_LANES)
    rn = _tile(tm, 128, V7X_SUBLANES)
    rc = _tile(tm, 64, BF16_SUBLANES)
    kern = functools.partial(_conv_ffn_kernel, tm=tm, dn=dn, rn=rn, rc=rc, nf=nf, final_norm=final_norm)
    vmem = (2 * (2 * d * tf * 2 + tf * d * 2 + tm * d * 4) + tm * d * 4 + tm * d * 2
            + 12 * tm * tf * 4 + 4 * rn * d * 4 + tm * dn * 4 + 2 * MIB)
    chunk = lambda f, lag: jnp.clip(f - lag, 0, nf - 1)
    n_taps = conv_w.shape[0]
    taps_by_chunk = conv_w.astype(F32).reshape(n_taps, nf, tf).transpose(1, 0, 2)
    bias_by_chunk = conv_b.astype(F32).reshape(nf, 1, tf)
    return pl.pallas_call(
        kern,
        grid=(s // tm, nf + 2),
        in_specs=[
            pl.BlockSpec((tm, d), lambda i, f: (i, 0), pipeline_mode=pl.Buffered(1)),
            pl.BlockSpec((1, d), lambda i, f: (0, 0)),
            pl.BlockSpec((1, d), lambda i, f: (0, scale_idx)),
            pl.BlockSpec((1, d), lambda i, f: (0, shift_idx)),
            pl.BlockSpec((d, tf), lambda i, f: (0, chunk(f, 0))),
            pl.BlockSpec((d, tf), lambda i, f: (0, nf + chunk(f, 0))),
            pl.BlockSpec((nf, n_taps, tf), lambda i, f: (0, 0, 0)),
            pl.BlockSpec((nf, 1, tf), lambda i, f: (0, 0, 0)),
            pl.BlockSpec((tf, d), lambda i, f: (chunk(f, 2), 0)),
            pl.BlockSpec((1, d), lambda i, f: (0, gate_idx)),
            pl.BlockSpec((1, d), lambda i, f: (0, 0)),
        ],
        out_specs=pl.BlockSpec((tm, d), lambda i, f: (i, 0)),
        out_shape=jax.ShapeDtypeStruct((s, d), F32),
        scratch_shapes=[
            pltpu.VMEM((tm, d), BF16),
            pltpu.VMEM((2, tm + V7X_SUBLANES, tf), F32),
            pltpu.VMEM((2, tm, tf), F32),
            pltpu.VMEM((2, tm, tf), BF16),
            pltpu.VMEM((nf, V7X_SUBLANES, tf), F32),
        ],
        compiler_params=_cparams(("arbitrary", "arbitrary"), vmem),
        name="conv_ffn",
    )(x, norm2_w.reshape(1, -1).astype(F32), mod, mod, w_up, w_up, taps_by_chunk, bias_by_chunk, w_down, mod,
      norm_w.reshape(1, -1).astype(F32))


def kernel(x, c, w_ada, b_ada, norm1_w, w_in, lambda_q1, lambda_k1, lambda_q2, lambda_k2, subln_w, conv_w, conv_b, w_rg_a, b_rg_a, w_rg_x, b_rg_x, lru_lambda, w_proj_attn, w_proj_rec, w_out, norm2_w, w_ffn_up, ffn_conv_w, ffn_conv_b, w_ffn_down, norm_f_w):
    batch, seq, d = x.shape
    depth = w_ada.shape[0]
    dh = lambda_q1.shape[-1]
    attn_w = w_proj_attn.shape[1]
    n_heads = attn_w // (2 * dh)
    d_rnn = conv_w.shape[-1]
    assert w_in.shape[-1] == 3 * attn_w + 2 * d_rnn + 2 * d
    row = lambda a: a.reshape(1, -1).astype(F32)

    outs = []
    for b in range(batch):
        xb = x[b]
        cb = c[b:b + 1]
        for l in range(depth):
            lambda_init = 0.8 - 0.6 * math.exp(-0.3 * l)
            mod = _adaln_mod(cb, w_ada[l], row(b_ada[l]))
            h1 = _norm_modulate(xb, row(norm1_w[l]), mod, 0, 1)
            q_scale = jnp.concatenate([jnp.full((1, attn_w), dh ** -0.5 * LOG2E, F32),
                                       jnp.ones((1, 2 * attn_w), F32)], axis=1)
            qkv = _matmul(h1, w_in[l], 0, 3 * attn_w, BF16, q_scale)
            rest = _matmul(h1, w_in[l], 3 * attn_w, 2 * d_rnn + 2 * d, F32)
            y_att = _diff_attention(qkv, lambda_q1[l], lambda_k1[l], lambda_q2[l], lambda_k2[l], subln_w[l],
                                    lambda_init, n_heads, dh)
            y_rec = _rg_lru(rest, conv_w[l], conv_b[l], w_rg_a[l], b_rg_a[l], w_rg_x[l], b_rg_x[l],
                            lru_lambda[l], d_rnn)
            mixed = _gated_merge(y_att, y_rec, w_proj_attn[l], w_proj_rec[l], rest, 2 * d_rnn, 2 * d_rnn + d)
            x1 = _matmul_residual(mixed, w_out[l], xb, mod, 2)
            last = l == depth - 1
            xb = _conv_ffn(x1, norm2_w[l], mod, 3, 4, 5, w_ffn_up[l].astype(BF16), ffn_conv_w[l], ffn_conv_b[l],
                           w_ffn_down[l].astype(BF16), norm_f_w, final_norm=last)
        outs.append(xb)
    return outs[0][None] if batch == 1 else jnp.stack(outs, axis=0)
```

```python
import functools
import math

import jax
import jax.numpy as jnp
from jax import lax
from jax.experimental import pallas as pl
from jax.experimental.pallas import tpu as pltpu

F32 = jnp.float32
BF16 = jnp.bfloat16
EPS = 1e-6
RG_C = 8.0
NEG_BIG = -1e30

V7X_LANES = 128
V7X_SUBLANES = 8
BF16_SUBLANES = 16
V7X_VMEM_LIMIT_BYTES = 60000 * 1024
MIB = 1024 * 1024


def _cparams(semantics, vmem_bytes):
    return pltpu.CompilerParams(
        dimension_semantics=semantics,
        vmem_limit_bytes=int(min(vmem_bytes, V7X_VMEM_LIMIT_BYTES)),
    )


def _tile(n, want, quantum):
    t = min(n, want) // quantum * quantum
    while t > quantum and n % t:
        t -= quantum
    assert t >= quantum and n % t == 0, (n, want, quantum)
    return t


def _rms_scale(x):
    return lax.rsqrt(jnp.mean(x * x, axis=-1, keepdims=True) + EPS)


def _adaln_mod_kernel(c_ref, w_ref, b_ref, o_ref):
    c = c_ref[...]
    a = c * jax.nn.sigmoid(c)
    a8 = jnp.broadcast_to(a, (V7X_SUBLANES, a.shape[1])).astype(BF16)
    acc = jnp.dot(a8, w_ref[...].astype(BF16), preferred_element_type=F32)
    o_ref[...] = acc[0:1, :] + b_ref[...]


def _adaln_mod(c, w, b):
    d, n = w.shape
    tn = _tile(n, 1024, V7X_LANES)
    return pl.pallas_call(
        _adaln_mod_kernel,
        grid=(n // tn,),
        in_specs=[
            pl.BlockSpec((1, d), lambda j: (0, 0)),
            pl.BlockSpec((d, tn), lambda j: (0, j)),
            pl.BlockSpec((1, tn), lambda j: (0, j)),
        ],
        out_specs=pl.BlockSpec((1, tn), lambda j: (0, j)),
        out_shape=jax.ShapeDtypeStruct((1, n), F32),
        compiler_params=_cparams(("arbitrary",), 2 * d * tn * 4 + d * tn * 2 + 4 * MIB),
        name="adaln_mod",
    )(c, w, b)


def _norm_modulate_kernel(x_ref, w_ref, scale_ref, shift_ref, o_ref):
    x = x_ref[...]
    y = x * _rms_scale(x) * w_ref[...]
    o_ref[...] = (y * (1.0 + scale_ref[...]) + shift_ref[...]).astype(o_ref.dtype)


def _norm_modulate(x, w, mod, shift_idx, scale_idx):
    s, d = x.shape
    tm = _tile(s, 256, BF16_SUBLANES)
    return pl.pallas_call(
        _norm_modulate_kernel,
        grid=(s // tm,),
        in_specs=[
            pl.BlockSpec((tm, d), lambda i: (i, 0)),
            pl.BlockSpec((1, d), lambda i: (0, 0)),
            pl.BlockSpec((1, d), lambda i: (0, scale_idx)),
            pl.BlockSpec((1, d), lambda i: (0, shift_idx)),
        ],
        out_specs=pl.BlockSpec((tm, d), lambda i: (i, 0)),
        out_shape=jax.ShapeDtypeStruct((s, d), BF16),
        compiler_params=_cparams(("parallel",), 2 * tm * d * 6 + 3 * tm * d * 4 + 2 * MIB),
        name="norm_modulate",
    )(x, w, mod, mod)


def _cast_weight_once(w_ref, wb_ref):
    @pl.when(pl.program_id(1) == 0)
    def _():
        wb_ref[...] = w_ref[...].astype(BF16)


def _matmul_kernel(x_ref, w_ref, o_ref, wb_ref):
    _cast_weight_once(w_ref, wb_ref)
    o_ref[...] = jnp.dot(x_ref[...], wb_ref[...], preferred_element_type=F32).astype(o_ref.dtype)


def _matmul_colscale_kernel(x_ref, w_ref, s_ref, o_ref, wb_ref):
    _cast_weight_once(w_ref, wb_ref)
    y = jnp.dot(x_ref[...], wb_ref[...], preferred_element_type=F32)
    o_ref[...] = (y * s_ref[...]).astype(o_ref.dtype)


def _matmul(x, w, col0, n, out_dtype, col_scale=None):
    m, k = x.shape
    tm = _tile(m, 1024, BF16_SUBLANES)
    tn = _tile(math.gcd(n, col0) if col0 else n, 512, V7X_LANES)
    ob = jnp.dtype(out_dtype).itemsize
    vmem = 2 * (tm * k * 2 + k * tn * 4 + tm * tn * ob) + k * tn * 2 + 2 * tm * tn * 4 + 2 * MIB
    in_specs = [
        pl.BlockSpec((tm, k), lambda j, i: (i, 0)),
        pl.BlockSpec((k, tn), lambda j, i: (0, col0 // tn + j)),
    ]
    args = (x, w)
    if col_scale is not None:
        in_specs.append(pl.BlockSpec((1, tn), lambda j, i: (0, j)))
        args = (x, w, col_scale)
    return pl.pallas_call(
        _matmul_kernel if col_scale is None else _matmul_colscale_kernel,
        grid=(n // tn, m // tm),
        in_specs=in_specs,
        out_specs=pl.BlockSpec((tm, tn), lambda j, i: (i, j)),
        out_shape=jax.ShapeDtypeStruct((m, n), out_dtype),
        scratch_shapes=[pltpu.VMEM((k, tn), BF16)],
        compiler_params=_cparams(("arbitrary", "arbitrary"), vmem),
        name="matmul",
    )(*args)


LOG2E = math.log2(math.e)
POS_RADIX = 32
N_BIAS_COLS = 6


def _split3_bf16(x):
    hi = x.astype(BF16).astype(F32)
    mid = (x - hi).astype(BF16).astype(F32)
    lo = (x - hi - mid).astype(BF16).astype(F32)
    return hi, mid, lo


def _diff_attn_kernel(q_ref, k_ref, v_ref, pos_ref, ones_ref, tri_ref, qc_ref, sh_ref, lq1_ref, lk1_ref, lq2_ref, lk2_ref,
                      sw_ref, o_ref, ka_ref, va_ref, m_ref, acc_ref, z_ref, *, tq, tk, dh, lambda_init):
    i = pl.program_id(1)
    hw = 2 * dh
    s_len = k_ref.shape[0]

    @pl.when(i == 0)
    def _():
        lane_k = lax.broadcasted_iota(jnp.int32, (tk, hw), 1)

        def prep(j, carry):
            rows = pl.ds(pl.multiple_of(j * tk, tk), tk)
            kj = k_ref[rows, :]
            ka_ref[0, rows, :] = jnp.where(lane_k < dh, kj, pos_ref[0])
            ka_ref[1, rows, :] = jnp.where(lane_k >= dh, kj, pos_ref[1])
            va_ref[rows, 0:hw] = v_ref[rows, :]
            va_ref[rows, hw:2 * hw] = ones_ref[...]
            return carry

        lax.fori_loop(0, s_len // tk, prep, 0)

    lane_q = lax.broadcasted_iota(jnp.int32, (tq, hw), 1)
    qc = qc_ref[0].astype(BF16)

    def score_operands(q):
        return (jnp.where(lane_q < dh, q, qc[0:1, :]), jnp.where(lane_q >= dh, q, qc[1:2, :]))

    q_maps = score_operands(q_ref[...])
    frame_shift = sh_ref[0][:, 0:1]

    m_ref[...] = jnp.full(m_ref.shape, NEG_BIG, F32)
    acc_ref[...] = jnp.zeros(acc_ref.shape, F32)

    def scores(maps, j, c):
        rows = pl.ds(pl.multiple_of(j * tk, tk), tk)
        return lax.dot_general(maps[c], ka_ref[c, rows, :], (((1,), (1,)), ((), ())),
                               preferred_element_type=F32)

    def lanes(x, width):
        return jnp.concatenate([x] * (width // V7X_LANES), axis=-1)

    def update(j, c, z):
        width = z.shape[1]
        rows = pl.ds(pl.multiple_of(j * tk, tk), width)
        m_prev = m_ref[c] - frame_shift
        m_new = jnp.maximum(m_prev, jnp.max(z, axis=-1, keepdims=True))
        alpha = jnp.exp2(m_prev - m_new)
        p = jnp.exp2(z - lanes(m_new, width)).astype(BF16)
        acc_ref[c] = lanes(alpha, 2 * hw) * acc_ref[c] + jnp.dot(p, va_ref[rows, :], preferred_element_type=F32)
        m_ref[c] = m_new

    n_full = (i * tq) // tk

    def step(j, cur):
        for c in range(2):
            z_ref[1 - cur, c] = scores(q_maps, j + 1, c)
        for c in range(2):
            update(j, c, z_ref[cur, c])

    def diagonal_step(cur):
        for pos in range(tk // tq):
            @pl.when(i % (tk // tq) == pos)
            def _():
                lo, width = pos * tq, (pos + 1) * tq
                for c in range(2):
                    diag = z_ref[cur, c, :, lo:width] + tri_ref[...]
                    z = diag if pos == 0 else jnp.concatenate([z_ref[cur, c, :, 0:lo], diag], axis=-1)
                    update(n_full, c, z)

    for c in range(2):
        z_ref[0, c] = scores(q_maps, 0, c)

    def chunk_pair(t, carry):
        step(2 * t, 0)
        step(2 * t + 1, 1)
        return carry

    lax.fori_loop(0, n_full // 2, chunk_pair, 0)

    @pl.when(n_full % 2 == 1)
    def _():
        step(n_full - 1, 0)
        diagonal_step(1)

    @pl.when(n_full % 2 == 0)
    def _():
        diagonal_step(0)

    a0 = acc_ref[0]
    a1 = acc_ref[1]
    o1 = a0[:, 0:hw] / a0[:, hw:2 * hw]
    o2 = a1[:, 0:hw] / a1[:, hw:2 * hw]
    lam = (jnp.exp(jnp.sum(lq1_ref[...] * lk1_ref[...], axis=-1, keepdims=True))
           - jnp.exp(jnp.sum(lq2_ref[...] * lk2_ref[...], axis=-1, keepdims=True)) + lambda_init)
    o = o1 - lam * o2
    y = o * _rms_scale(o) * sw_ref[...]
    o_ref[...] = (y * (1.0 - lambda_init)).astype(o_ref.dtype)


def _diff_attention(qkv, lq1, lk1, lq2, lk2, subln_w, lambda_init, n_heads, dh):
    s = qkv.shape[0]
    hw = 2 * dh
    tk = _tile(s, 1024, V7X_LANES)
    tq = _tile(tk, 512, BF16_SUBLANES)
    nb = N_BIAS_COLS
    assert dh >= nb and tk // POS_RADIX <= 256
    slopes2 = (2.0 ** (-8.0 * jnp.arange(1, n_heads + 1, dtype=F32) / n_heads)) * LOG2E
    pieces = jnp.stack(_split3_bf16(slopes2), axis=-1)
    q_cols = jnp.concatenate([pieces * POS_RADIX, pieces], axis=-1)
    qc = jnp.zeros((n_heads, 2, hw), F32).at[:, 0, dh:dh + nb].set(q_cols).at[:, 1, 0:nb].set(q_cols)
    jj = jnp.arange(tk, dtype=jnp.int32)
    k_cols = jnp.stack([jj // POS_RADIX] * 3 + [jj % POS_RADIX] * 3, axis=-1).astype(F32)
    pos = jnp.zeros((2, tk, hw), F32).at[0, :, dh:dh + nb].set(k_cols).at[1, :, 0:nb].set(k_cols).astype(BF16)
    ones = jnp.ones((tk, hw), BF16)
    shift = jnp.broadcast_to((slopes2 * tk)[:, None, None], (n_heads, 1, V7X_LANES))
    rel = jnp.arange(tq, dtype=jnp.int32)[None, :] - jnp.arange(tq, dtype=jnp.int32)[:, None]
    tri_bias = jnp.where(rel <= 0, 0.0, NEG_BIG).astype(F32)
    kern = functools.partial(_diff_attn_kernel, tq=tq, tk=tk, dh=dh, lambda_init=lambda_init)
    vec = lambda a: a.reshape(1, -1).astype(F32)
    const2 = lambda h, i: (0, 0)
    vmem = (2 * (2 * s * hw * 2 + 2 * tq * hw * 2 + 3 * tk * hw * 2) + 4 * s * hw * 2
            + 8 * tq * tk * 4 + 2 * tq * (2 * hw + V7X_LANES) * 4 + 4 * MIB)
    return pl.pallas_call(
        kern,
        grid=(n_heads, s // tq),
        in_specs=[
            pl.BlockSpec((tq, hw), lambda h, i: (i, h)),
            pl.BlockSpec((s, hw), lambda h, i: (0, n_heads + h)),
            pl.BlockSpec((s, hw), lambda h, i: (0, 2 * n_heads + h)),
            pl.BlockSpec((2, tk, hw), lambda h, i: (0, 0, 0)),
            pl.BlockSpec((tk, hw), const2),
            pl.BlockSpec((tq, tq), const2),
            pl.BlockSpec((1, 2, hw), lambda h, i: (h, 0, 0)),
            pl.BlockSpec((1, 1, V7X_LANES), lambda h, i: (h, 0, 0)),
            pl.BlockSpec((1, dh), const2),
            pl.BlockSpec((1, dh), const2),
            pl.BlockSpec((1, dh), const2),
            pl.BlockSpec((1, dh), const2),
            pl.BlockSpec((1, hw), const2),
        ],
        out_specs=pl.BlockSpec((tq, hw), lambda h, i: (i, h)),
        out_shape=jax.ShapeDtypeStruct((s, n_heads * hw), BF16),
        scratch_shapes=[
            pltpu.VMEM((2, s, hw), BF16),
            pltpu.VMEM((s, 2 * hw), BF16),
            pltpu.VMEM((2, tq, V7X_LANES), F32),
            pltpu.VMEM((2, tq, 2 * hw), F32),
            pltpu.VMEM((2, 2, tq, tk), F32),
        ],
        compiler_params=_cparams(("arbitrary", "arbitrary"), vmem),
        name="diff_attention",
    )(qkv, qkv, qkv, pos, ones, tri_bias, qc, shift, vec(lq1), vec(lk1), vec(lq2), vec(lk2), vec(subln_w))


def _rg_lru_kernel(rx_ref, rg_ref, cw_ref, cb_ref, wa_ref, ba_ref, wx_ref, bx_ref, lam_ref, o_ref,
                   ext_ref, a_ref, b_ref, tail_ref, h_ref, *, t_rows, cw, blk):
    t = pl.program_id(1)
    halo = V7X_SUBLANES

    @pl.when(t == 0)
    def _():
        tail_ref[...] = jnp.zeros(tail_ref.shape, F32)
        h_ref[...] = jnp.zeros(h_ref.shape, F32)

    rx = rx_ref[...]
    ext_ref[0:halo, :] = tail_ref[...]
    ext_ref[halo:, :] = rx
    tail_ref[...] = rx[t_rows - halo:, :]
    ext = ext_ref[...]
    taps = cw_ref[...]
    n_taps = taps.shape[0]
    xr = cb_ref[...]
    for j in range(n_taps):
        lag = n_taps - 1 - j
        xs = rx if lag == 0 else pltpu.roll(ext, lag, 0)[halo:, :]
        xr = xr + xs * taps[j:j + 1, :]

    ra, ia = [], []
    for g in range(cw // blk):
        xb = xr[:, g * blk:(g + 1) * blk].astype(BF16)
        ra.append(jnp.dot(xb, wa_ref[g], preferred_element_type=F32))
        ia.append(jnp.dot(xb, wx_ref[g], preferred_element_type=F32))
    r = jax.nn.sigmoid(jnp.concatenate(ra, axis=-1) + ba_ref[...])
    ig = jax.nn.sigmoid(jnp.concatenate(ia, axis=-1) + bx_ref[...])
    neg_lam = -lam_ref[...]
    softplus = jnp.maximum(neg_lam, 0.0) + jnp.log1p(jnp.exp(-jnp.abs(neg_lam)))
    log_a = (-RG_C) * r * softplus
    th = jnp.tanh(log_a)
    a_ref[...] = jnp.exp(log_a)
    u = -2.0 * th / (1.0 - th)
    b_ref[...] = jnp.where(u > 0.0, u * lax.rsqrt(u), 0.0) * (ig * xr)

    row = lax.broadcasted_iota(jnp.int32, (V7X_SUBLANES, cw), 0)

    def step(n, h):
        rows = pl.ds(pl.multiple_of(n * V7X_SUBLANES, V7X_SUBLANES), V7X_SUBLANES)
        a = a_ref[rows, :]
        b = b_ref[rows, :]
        d = 1
        while d < V7X_SUBLANES:
            keep = row >= d
            a_s = jnp.where(keep, pltpu.roll(a, d, 0), 1.0)
            b_s = jnp.where(keep, pltpu.roll(b, d, 0), 0.0)
            b = a * b_s + b
            a = a * a_s
            d *= 2
        hh = a * h + b
        b_ref[rows, :] = hh
        return hh[V7X_SUBLANES - 1:, :]

    h_last = lax.fori_loop(0, t_rows // V7X_SUBLANES, step, h_ref[0:1, :], unroll=4)
    h_ref[...] = jnp.broadcast_to(h_last, h_ref.shape)
    o_ref[...] = (b_ref[...] * jax.nn.gelu(rg_ref[...])).astype(o_ref.dtype)


def _rg_lru(rest, conv_w, conv_b, w_a, b_a, w_x, b_x, lru_lambda, d_rnn):
    s = rest.shape[0]
    n_blk, blk, _ = w_a.shape
    cw = _tile(d_rnn, 512, blk)
    t_rows = _tile(s, 1024, BF16_SUBLANES)
    n_cb = d_rnn // cw
    row = lambda a: a.reshape(1, -1).astype(F32)
    kern = functools.partial(_rg_lru_kernel, t_rows=t_rows, cw=cw, blk=blk)
    vec_spec = pl.BlockSpec((1, cw), lambda n, t: (0, n))
    gate_spec = pl.BlockSpec((cw // blk, blk, blk), lambda n, t: (n, 0, 0))
    vmem = 2 * (2 * t_rows * cw * 4 + t_rows * cw * 2) + 12 * t_rows * cw * 4 + 4 * MIB
    return pl.pallas_call(
        kern,
        grid=(n_cb, s // t_rows),
        in_specs=[
            pl.BlockSpec((t_rows, cw), lambda n, t: (t, n)),
            pl.BlockSpec((t_rows, cw), lambda n, t: (t, n_cb + n)),
            pl.BlockSpec((conv_w.shape[0], cw), lambda n, t: (0, n)),
            vec_spec, gate_spec, vec_spec, gate_spec, vec_spec, vec_spec,
        ],
        out_specs=pl.BlockSpec((t_rows, cw), lambda n, t: (t, n)),
        out_shape=jax.ShapeDtypeStruct((s, d_rnn), BF16),
        scratch_shapes=[
            pltpu.VMEM((t_rows + V7X_SUBLANES, cw), F32),
            pltpu.VMEM((t_rows, cw), F32),
            pltpu.VMEM((t_rows, cw), F32),
            pltpu.VMEM((V7X_SUBLANES, cw), F32),
            pltpu.VMEM((V7X_SUBLANES, cw), F32),
        ],
        compiler_params=_cparams(("parallel", "arbitrary"), vmem),
        name="rg_lru",
    )(rest, rest, conv_w.astype(F32), row(conv_b), w_a.astype(BF16), row(b_a), w_x.astype(BF16), row(b_x),
      row(lru_lambda))


def _gated_merge_kernel(ya_ref, yr_ref, wa_ref, wr_ref, ga_ref, gb_ref, o_ref, wab_ref, wrb_ref):
    _cast_weight_once(wa_ref, wab_ref)
    _cast_weight_once(wr_ref, wrb_ref)
    pa = jnp.dot(ya_ref[...], wab_ref[...], preferred_element_type=F32)
    pr = jnp.dot(yr_ref[...], wrb_ref[...], preferred_element_type=F32)
    o_ref[...] = (jax.nn.sigmoid(ga_ref[...]) * pa + jax.nn.sigmoid(gb_ref[...]) * pr).astype(o_ref.dtype)


def _gated_merge(y_att, y_rec, w_pa, w_pr, rest, ga_col, gb_col):
    m, ka = y_att.shape
    _, kr = y_rec.shape
    n = w_pa.shape[1]
    tm = _tile(m, 1024, BF16_SUBLANES)
    tn = _tile(math.gcd(math.gcd(n, ga_col), gb_col), 512, V7X_LANES)
    vmem = (2 * (tm * (ka + kr) * 2 + (ka + kr) * tn * 4 + 2 * tm * tn * 4 + tm * tn * 2) + (ka + kr) * tn * 2
            + 4 * tm * tn * 4 + 2 * MIB)
    return pl.pallas_call(
        _gated_merge_kernel,
        grid=(n // tn, m // tm),
        in_specs=[
            pl.BlockSpec((tm, ka), lambda j, i: (i, 0)),
            pl.BlockSpec((tm, kr), lambda j, i: (i, 0)),
            pl.BlockSpec((ka, tn), lambda j, i: (0, j)),
            pl.BlockSpec((kr, tn), lambda j, i: (0, j)),
            pl.BlockSpec((tm, tn), lambda j, i: (i, ga_col // tn + j)),
            pl.BlockSpec((tm, tn), lambda j, i: (i, gb_col // tn + j)),
        ],
        out_specs=pl.BlockSpec((tm, tn), lambda j, i: (i, j)),
        out_shape=jax.ShapeDtypeStruct((m, n), BF16),
        scratch_shapes=[pltpu.VMEM((ka, tn), BF16), pltpu.VMEM((kr, tn), BF16)],
        compiler_params=_cparams(("arbitrary", "arbitrary"), vmem),
        name="gated_merge",
    )(y_att, y_rec, w_pa, w_pr, rest, rest)


def _matmul_residual_kernel(a_ref, w_ref, x_ref, g_ref, o_ref, wb_ref):
    _cast_weight_once(w_ref, wb_ref)
    y = jnp.dot(a_ref[...], wb_ref[...], preferred_element_type=F32)
    o_ref[...] = x_ref[...] + g_ref[...] * y


def _matmul_residual(a, w, x, mod, gate_idx):
    m, k = a.shape
    n = w.shape[1]
    tm = _tile(m, 1024, BF16_SUBLANES)
    tn = _tile(n, 512, V7X_LANES)
    vmem = 2 * (tm * k * 2 + k * tn * 4 + 2 * tm * tn * 4) + k * tn * 2 + tm * tn * 4 + 2 * MIB
    return pl.pallas_call(
        _matmul_residual_kernel,
        grid=(n // tn, m // tm),
        in_specs=[
            pl.BlockSpec((tm, k), lambda j, i: (i, 0)),
            pl.BlockSpec((k, tn), lambda j, i: (0, j)),
            pl.BlockSpec((tm, tn), lambda j, i: (i, j)),
            pl.BlockSpec((1, tn), lambda j, i: (0, gate_idx * (n // tn) + j)),
        ],
        out_specs=pl.BlockSpec((tm, tn), lambda j, i: (i, j)),
        out_shape=jax.ShapeDtypeStruct((m, n), F32),
        scratch_shapes=[pltpu.VMEM((k, tn), BF16)],
        compiler_params=_cparams(("arbitrary", "arbitrary"), vmem),
        name="matmul_residual",
    )(a, w, x, mod)


def _conv_ffn_kernel(h_ref, wg_ref, wu_ref, cw_ref, cb_ref, wd_ref, x_ref, g_ref, nw_ref, o_ref,
                     gate_ref, up_ref, act_ref, carry_ref, *, tm, dn, rn, rc, nf, final_norm):
    mi = pl.program_id(0)
    f = pl.program_id(1)
    halo = V7X_SUBLANES
    d = o_ref.shape[1]

    def gate_up(slot):
        @pl.when(mi == 0)
        def _():
            carry_ref[f] = jnp.zeros(carry_ref.shape[1:], F32)

        h = h_ref[...]
        gate = jnp.dot(h, wg_ref[...], preferred_element_type=F32)
        up_ref[slot] = jnp.dot(h, wu_ref[...], preferred_element_type=F32)
        gate_ref[slot, 0:halo, :] = carry_ref[f]
        gate_ref[slot, halo:, :] = gate
        carry_ref[f] = gate[tm - halo:, :]

    def activation_rows(slot, r):
        r0 = r * rc
        taps = cw_ref[f - 1]
        n_taps = taps.shape[0]
        gc = cb_ref[f - 1]
        for j in range(n_taps):
            start = halo - (n_taps - 1 - j) + r0
            gc = gc + gate_ref[slot, start:start + rc, :] * taps[j:j + 1, :]
        act_ref[slot, r0:r0 + rc, :] = (jax.nn.gelu(gc) * up_ref[slot, r0:r0 + rc, :]).astype(BF16)

    def down_cols(slot, c):
        cols = slice(c * dn, (c + 1) * dn)
        o_ref[:, cols] += jnp.dot(act_ref[slot], wd_ref[:, cols], preferred_element_type=F32)

    def step(gu_slot, act_slot, down_slot):
        n_c = d // dn
        n_r = tm // rc
        if gu_slot is not None:
            gate_up(gu_slot)
        for k in range(max(n_c, n_r)):
            if act_slot is not None and k < n_r:
                activation_rows(act_slot, k)
            if down_slot is not None and k < n_c:
                down_cols(down_slot, k)

    @pl.when(f == 0)
    def _():
        o_ref[...] = jnp.zeros(o_ref.shape, F32)
        step(0, None, None)

    @pl.when(f == 1)
    def _():
        step(1, 0, None)

    @pl.when(jnp.logical_and(jnp.logical_and(f >= 2, f < nf), f % 2 == 0))
    def _():
        step(0, 1, 0)

    @pl.when(jnp.logical_and(jnp.logical_and(f >= 2, f < nf), f % 2 == 1))
    def _():
        step(1, 0, 1)

    @pl.when(f == nf)
    def _():
        step(None, (nf - 1) % 2, nf % 2)

    @pl.when(f == nf + 1)
    def _():
        step(None, None, (nf - 1) % 2)
        for r in range(tm // rn):
            rows = slice(r * rn, (r + 1) * rn)
            x2 = x_ref[rows, :] + g_ref[...] * o_ref[rows, :]
            if final_norm:
                x2 = x2 * _rms_scale(x2) * nw_ref[...]
            o_ref[rows, :] = x2


def _conv_ffn(h, w_up, conv_w, conv_b, w_down, x, mod, gate_idx, norm_w, final_norm):
    s, d = h.shape
    d_ff = w_down.shape[0]
    tm = _tile(s, 512, BF16_SUBLANES)
    tf = _tile(d_ff, 256, V7X_LANES)
    nf = d_ff // tf
    assert nf >= 2
    dn = _tile(d, 512, V7X_LANES)
    rn = _tile(tm, 128, V7X_SUBLANES)
    rc = _tile(tm, 64, BF16_SUBLANES)
    kern = functools.partial(_conv_ffn_kernel, tm=tm, dn=dn, rn=rn, rc=rc, nf=nf, final_norm=final_norm)
    vmem = (2 * (tm * d * 2 + 2 * d * tf * 2 + tf * d * 2 + tm * d * 4) + tm * d * 4
            + 12 * tm * tf * 4 + 4 * rn * d * 4 + tm * dn * 4 + 2 * MIB)
    chunk = lambda f, lag: jnp.clip(f - lag, 0, nf - 1)
    n_taps = conv_w.shape[0]
    taps_by_chunk = conv_w.astype(F32).reshape(n_taps, nf, tf).transpose(1, 0, 2)
    bias_by_chunk = conv_b.astype(F32).reshape(nf, 1, tf)
    return pl.pallas_call(
        kern,
        grid=(s // tm, nf + 2),
        in_specs=[
            pl.BlockSpec((tm, d), lambda i, f: (i, 0)),
            pl.BlockSpec((d, tf), lambda i, f: (0, chunk(f, 0))),
            pl.BlockSpec((d, tf), lambda i, f: (0, nf + chunk(f, 0))),
            pl.BlockSpec((nf, n_taps, tf), lambda i, f: (0, 0, 0)),
            pl.BlockSpec((nf, 1, tf), lambda i, f: (0, 0, 0)),
            pl.BlockSpec((tf, d), lambda i, f: (chunk(f, 2), 0)),
            pl.BlockSpec((tm, d), lambda i, f: (i, 0), pipeline_mode=pl.Buffered(1)),
            pl.BlockSpec((1, d), lambda i, f: (0, gate_idx)),
            pl.BlockSpec((1, d), lambda i, f: (0, 0)),
        ],
        out_specs=pl.BlockSpec((tm, d), lambda i, f: (i, 0)),
        out_shape=jax.ShapeDtypeStruct((s, d), F32),
        scratch_shapes=[
            pltpu.VMEM((2, tm + V7X_SUBLANES, tf), F32),
            pltpu.VMEM((2, tm, tf), F32),
            pltpu.VMEM((2, tm, tf), BF16),
            pltpu.VMEM((nf, V7X_SUBLANES, tf), F32),
        ],
        compiler_params=_cparams(("arbitrary", "arbitrary"), vmem),
        name="conv_ffn",
    )(h, w_up, w_up, taps_by_chunk, bias_by_chunk, w_down, x, mod, norm_w.reshape(1, -1).astype(F32))


def kernel(x, c, w_ada, b_ada, norm1_w, w_in, lambda_q1, lambda_k1, lambda_q2, lambda_k2, subln_w, conv_w, conv_b, w_rg_a, b_rg_a, w_rg_x, b_rg_x, lru_lambda, w_proj_attn, w_proj_rec, w_out, norm2_w, w_ffn_up, ffn_conv_w, ffn_conv_b, w_ffn_down, norm_f_w):
    batch, seq, d = x.shape
    depth = w_ada.shape[0]
    dh = lambda_q1.shape[-1]
    attn_w = w_proj_attn.shape[1]
    n_heads = attn_w // (2 * dh)
    d_rnn = conv_w.shape[-1]
    assert w_in.shape[-1] == 3 * attn_w + 2 * d_rnn + 2 * d
    row = lambda a: a.reshape(1, -1).astype(F32)

    outs = []
    for b in range(batch):
        xb = x[b]
        cb = c[b:b + 1]
        for l in range(depth):
            lambda_init = 0.8 - 0.6 * math.exp(-0.3 * l)
            mod = _adaln_mod(cb, w_ada[l], row(b_ada[l]))
            h1 = _norm_modulate(xb, row(norm1_w[l]), mod, 0, 1)
            q_scale = jnp.concatenate([jnp.full((1, attn_w), dh ** -0.5 * LOG2E, F32),
                                       jnp.ones((1, 2 * attn_w), F32)], axis=1)
            qkv = _matmul(h1, w_in[l], 0, 3 * attn_w, BF16, q_scale)
            rest = _matmul(h1, w_in[l], 3 * attn_w, 2 * d_rnn + 2 * d, F32)
            y_att = _diff_attention(qkv, lambda_q1[l], lambda_k1[l], lambda_q2[l], lambda_k2[l], subln_w[l],
                                    lambda_init, n_heads, dh)
            y_rec = _rg_lru(rest, conv_w[l], conv_b[l], w_rg_a[l], b_rg_a[l], w_rg_x[l], b_rg_x[l],
                            lru_lambda[l], d_rnn)
            mixed = _gated_merge(y_att, y_rec, w_proj_attn[l], w_proj_rec[l], rest, 2 * d_rnn, 2 * d_rnn + d)
            x1 = _matmul_residual(mixed, w_out[l], xb, mod, 2)
            h2 = _norm_modulate(x1, row(norm2_w[l]), mod, 3, 4)
            last = l == depth - 1
            xb = _conv_ffn(h2, w_ffn_up[l].astype(BF16), ffn_conv_w[l], ffn_conv_b[l], w_ffn_down[l].astype(BF16),
                           x1, mod, 5, norm_f_w, final_norm=last)
        outs.append(xb)
    return outs[0][None] if batch == 1 else jnp.stack(outs, axis=0)
```

```python
import functools
import math

import jax
import jax.numpy as jnp
from jax import lax
from jax.experimental import pallas as pl
from jax.experimental.pallas import tpu as pltpu

F32 = jnp.float32
BF16 = jnp.bfloat16
EPS = 1e-6
RG_C = 8.0
NEG_BIG = -1e30

V7X_LANES = 128
V7X_SUBLANES = 8
BF16_SUBLANES = 16
V7X_VMEM_LIMIT_BYTES = 60000 * 1024
MIB = 1024 * 1024


def _cparams(semantics, vmem_bytes):
    return pltpu.CompilerParams(
        dimension_semantics=semantics,
        vmem_limit_bytes=int(min(vmem_bytes, V7X_VMEM_LIMIT_BYTES)),
    )


def _tile(n, want, quantum):
    t = min(n, want) // quantum * quantum
    while t > quantum and n % t:
        t -= quantum
    assert t >= quantum and n % t == 0, (n, want, quantum)
    return t


def _rms_scale(x):
    return lax.rsqrt(jnp.mean(x * x, axis=-1, keepdims=True) + EPS)


def _adaln_mod_kernel(c_ref, w_ref, b_ref, o_ref):
    c = c_ref[...]
    a = c * jax.nn.sigmoid(c)
    a8 = jnp.broadcast_to(a, (V7X_SUBLANES, a.shape[1])).astype(BF16)
    acc = jnp.dot(a8, w_ref[...].astype(BF16), preferred_element_type=F32)
    o_ref[...] = acc[0:1, :] + b_ref[...]


def _adaln_mod(c, w, b):
    d, n = w.shape
    tn = _tile(n, 1024, V7X_LANES)
    return pl.pallas_call(
        _adaln_mod_kernel,
        grid=(n // tn,),
        in_specs=[
            pl.BlockSpec((1, d), lambda j: (0, 0)),
            pl.BlockSpec((d, tn), lambda j: (0, j)),
            pl.BlockSpec((1, tn), lambda j: (0, j)),
        ],
        out_specs=pl.BlockSpec((1, tn), lambda j: (0, j)),
        out_shape=jax.ShapeDtypeStruct((1, n), F32),
        compiler_params=_cparams(("arbitrary",), 2 * d * tn * 4 + d * tn * 2 + 4 * MIB),
        name="adaln_mod",
    )(c, w, b)


def _norm_modulate_kernel(x_ref, w_ref, scale_ref, shift_ref, o_ref):
    x = x_ref[...]
    y = x * _rms_scale(x) * w_ref[...]
    o_ref[...] = (y * (1.0 + scale_ref[...]) + shift_ref[...]).astype(o_ref.dtype)


def _norm_modulate(x, w, mod, shift_idx, scale_idx):
    s, d = x.shape
    tm = _tile(s, 256, BF16_SUBLANES)
    return pl.pallas_call(
        _norm_modulate_kernel,
        grid=(s // tm,),
        in_specs=[
            pl.BlockSpec((tm, d), lambda i: (i, 0)),
            pl.BlockSpec((1, d), lambda i: (0, 0)),
            pl.BlockSpec((1, d), lambda i: (0, scale_idx)),
            pl.BlockSpec((1, d), lambda i: (0, shift_idx)),
        ],
        out_specs=pl.BlockSpec((tm, d), lambda i: (i, 0)),
        out_shape=jax.ShapeDtypeStruct((s, d), BF16),
        compiler_params=_cparams(("parallel",), 2 * tm * d * 6 + 3 * tm * d * 4 + 2 * MIB),
        name="norm_modulate",
    )(x, w, mod, mod)


def _cast_weight_once(w_ref, wb_ref):
    @pl.when(pl.program_id(1) == 0)
    def _():
        wb_ref[...] = w_ref[...].astype(BF16)


def _matmul_kernel(x_ref, w_ref, o_ref, wb_ref):
    _cast_weight_once(w_ref, wb_ref)
    o_ref[...] = jnp.dot(x_ref[...], wb_ref[...], preferred_element_type=F32).astype(o_ref.dtype)


def _matmul_colscale_kernel(x_ref, w_ref, s_ref, o_ref, wb_ref):
    _cast_weight_once(w_ref, wb_ref)
    y = jnp.dot(x_ref[...], wb_ref[...], preferred_element_type=F32)
    o_ref[...] = (y * s_ref[...]).astype(o_ref.dtype)


def _matmul(x, w, col0, n, out_dtype, col_scale=None):
    m, k = x.shape
    tm = _tile(m, 1024, BF16_SUBLANES)
    tn = _tile(math.gcd(n, col0) if col0 else n, 512, V7X_LANES)
    ob = jnp.dtype(out_dtype).itemsize
    vmem = 2 * (tm * k * 2 + k * tn * 4 + tm * tn * ob) + k * tn * 2 + 2 * tm * tn * 4 + 2 * MIB
    in_specs = [
        pl.BlockSpec((tm, k), lambda j, i: (i, 0)),
        pl.BlockSpec((k, tn), lambda j, i: (0, col0 // tn + j)),
    ]
    args = (x, w)
    if col_scale is not None:
        in_specs.append(pl.BlockSpec((1, tn), lambda j, i: (0, j)))
        args = (x, w, col_scale)
    return pl.pallas_call(
        _matmul_kernel if col_scale is None else _matmul_colscale_kernel,
        grid=(n // tn, m // tm),
        in_specs=in_specs,
        out_specs=pl.BlockSpec((tm, tn), lambda j, i: (i, j)),
        out_shape=jax.ShapeDtypeStruct((m, n), out_dtype),
        scratch_shapes=[pltpu.VMEM((k, tn), BF16)],
        compiler_params=_cparams(("arbitrary", "arbitrary"), vmem),
        name="matmul",
    )(*args)


LOG2E = math.log2(math.e)
POS_RADIX = 32
N_BIAS_COLS = 6


def _split3_bf16(x):
    hi = x.astype(BF16).astype(F32)
    mid = (x - hi).astype(BF16).astype(F32)
    lo = (x - hi - mid).astype(BF16).astype(F32)
    return hi, mid, lo


def _diff_attn_kernel(q_ref, k_ref, v_ref, pos_ref, ones_ref, tri_ref, qc_ref, sh_ref, lq1_ref, lk1_ref, lq2_ref, lk2_ref,
                      sw_ref, o_ref, ka_ref, va_ref, m_ref, acc_ref, z_ref, zmax_ref, *, tq, tk, dh, lambda_init):
    i = pl.program_id(1)
    hw = 2 * dh
    s_len = k_ref.shape[0]

    @pl.when(i == 0)
    def _():
        lane_k = lax.broadcasted_iota(jnp.int32, (tk, hw), 1)

        def prep(j, carry):
            rows = pl.ds(pl.multiple_of(j * tk, tk), tk)
            kj = k_ref[rows, :]
            ka_ref[0, rows, :] = jnp.where(lane_k < dh, kj, pos_ref[0])
            ka_ref[1, rows, :] = jnp.where(lane_k >= dh, kj, pos_ref[1])
            va_ref[rows, 0:hw] = v_ref[rows, :]
            va_ref[rows, hw:2 * hw] = ones_ref[...]
            return carry

        lax.fori_loop(0, s_len // tk, prep, 0)

    lane_q = lax.broadcasted_iota(jnp.int32, (tq, hw), 1)
    qc = qc_ref[0].astype(BF16)

    def score_operands(q):
        return (jnp.where(lane_q < dh, q, qc[0:1, :]), jnp.where(lane_q >= dh, q, qc[1:2, :]))

    q_maps = score_operands(q_ref[...])
    frame_shift = sh_ref[0][:, 0:1]

    m_ref[...] = jnp.full(m_ref.shape, NEG_BIG, F32)
    acc_ref[...] = jnp.zeros(acc_ref.shape, F32)

    def scores(maps, j, c):
        rows = pl.ds(pl.multiple_of(j * tk, tk), tk)
        return lax.dot_general(maps[c], ka_ref[c, rows, :], (((1,), (1,)), ((), ())),
                               preferred_element_type=F32)

    def lanes(x, width):
        return jnp.concatenate([x] * (width // V7X_LANES), axis=-1)

    def row_max(z):
        return jnp.broadcast_to(jnp.max(z, axis=-1, keepdims=True), (tq, V7X_LANES))

    def update(j, c, z, z_max):
        width = z.shape[1]
        rows = pl.ds(pl.multiple_of(j * tk, tk), width)
        m_prev = m_ref[c] - frame_shift
        m_new = jnp.maximum(m_prev, z_max)
        alpha = jnp.exp2(m_prev - m_new)
        p = jnp.exp2(z - lanes(m_new, width)).astype(BF16)
        acc_ref[c] = lanes(alpha, 2 * hw) * acc_ref[c] + jnp.dot(p, va_ref[rows, :], preferred_element_type=F32)
        m_ref[c] = m_new

    n_full = (i * tq) // tk

    def issue_scores(j, slot):
        for c in range(2):
            z = scores(q_maps, j, c)
            z_ref[slot, c] = z
            zmax_ref[slot, c] = row_max(z)

    def step(j, cur):
        issue_scores(j + 1, 1 - cur)
        for c in range(2):
            update(j, c, z_ref[cur, c], zmax_ref[cur, c])

    def diagonal_step(cur):
        for pos in range(tk // tq):
            @pl.when(i % (tk // tq) == pos)
            def _():
                lo, width = pos * tq, (pos + 1) * tq
                for c in range(2):
                    diag = z_ref[cur, c, :, lo:width] + tri_ref[...]
                    z = diag if pos == 0 else jnp.concatenate([z_ref[cur, c, :, 0:lo], diag], axis=-1)
                    update(n_full, c, z, row_max(z))

    issue_scores(0, 0)

    def chunk_pair(t, carry):
        step(2 * t, 0)
        step(2 * t + 1, 1)
        return carry

    lax.fori_loop(0, n_full // 2, chunk_pair, 0)

    @pl.when(n_full % 2 == 1)
    def _():
        step(n_full - 1, 0)
        diagonal_step(1)

    @pl.when(n_full % 2 == 0)
    def _():
        diagonal_step(0)

    a0 = acc_ref[0]
    a1 = acc_ref[1]
    o1 = a0[:, 0:hw] / a0[:, hw:2 * hw]
    o2 = a1[:, 0:hw] / a1[:, hw:2 * hw]
    lam = (jnp.exp(jnp.sum(lq1_ref[...] * lk1_ref[...], axis=-1, keepdims=True))
           - jnp.exp(jnp.sum(lq2_ref[...] * lk2_ref[...], axis=-1, keepdims=True)) + lambda_init)
    o = o1 - lam * o2
    y = o * _rms_scale(o) * sw_ref[...]
    o_ref[...] = (y * (1.0 - lambda_init)).astype(o_ref.dtype)


def _diff_attention(qkv, lq1, lk1, lq2, lk2, subln_w, lambda_init, n_heads, dh):
    s = qkv.shape[0]
    hw = 2 * dh
    tk = _tile(s, 1024, V7X_LANES)
    tq = _tile(tk, 512, BF16_SUBLANES)
    nb = N_BIAS_COLS
    assert dh >= nb and tk // POS_RADIX <= 256
    slopes2 = (2.0 ** (-8.0 * jnp.arange(1, n_heads + 1, dtype=F32) / n_heads)) * LOG2E
    pieces = jnp.stack(_split3_bf16(slopes2), axis=-1)
    q_cols = jnp.concatenate([pieces * POS_RADIX, pieces], axis=-1)
    qc = jnp.zeros((n_heads, 2, hw), F32).at[:, 0, dh:dh + nb].set(q_cols).at[:, 1, 0:nb].set(q_cols)
    jj = jnp.arange(tk, dtype=jnp.int32)
    k_cols = jnp.stack([jj // POS_RADIX] * 3 + [jj % POS_RADIX] * 3, axis=-1).astype(F32)
    pos = jnp.zeros((2, tk, hw), F32).at[0, :, dh:dh + nb].set(k_cols).at[1, :, 0:nb].set(k_cols).astype(BF16)
    ones = jnp.ones((tk, hw), BF16)
    shift = jnp.broadcast_to((slopes2 * tk)[:, None, None], (n_heads, 1, V7X_LANES))
    rel = jnp.arange(tq, dtype=jnp.int32)[None, :] - jnp.arange(tq, dtype=jnp.int32)[:, None]
    tri_bias = jnp.where(rel <= 0, 0.0, NEG_BIG).astype(F32)
    kern = functools.partial(_diff_attn_kernel, tq=tq, tk=tk, dh=dh, lambda_init=lambda_init)
    vec = lambda a: a.reshape(1, -1).astype(F32)
    const2 = lambda h, i: (0, 0)
    vmem = (2 * (2 * s * hw * 2 + 2 * tq * hw * 2 + 3 * tk * hw * 2) + 4 * s * hw * 2
            + 8 * tq * tk * 4 + 2 * tq * (2 * hw + V7X_LANES) * 4 + 4 * MIB)
    return pl.pallas_call(
        kern,
        grid=(n_heads, s // tq),
        in_specs=[
            pl.BlockSpec((tq, hw), lambda h, i: (i, h)),
            pl.BlockSpec((s, hw), lambda h, i: (0, n_heads + h)),
            pl.BlockSpec((s, hw), lambda h, i: (0, 2 * n_heads + h)),
            pl.BlockSpec((2, tk, hw), lambda h, i: (0, 0, 0)),
            pl.BlockSpec((tk, hw), const2),
            pl.BlockSpec((tq, tq), const2),
            pl.BlockSpec((1, 2, hw), lambda h, i: (h, 0, 0)),
            pl.BlockSpec((1, 1, V7X_LANES), lambda h, i: (h, 0, 0)),
            pl.BlockSpec((1, dh), const2),
            pl.BlockSpec((1, dh), const2),
            pl.BlockSpec((1, dh), const2),
            pl.BlockSpec((1, dh), const2),
            pl.BlockSpec((1, hw), const2),
        ],
        out_specs=pl.BlockSpec((tq, hw), lambda h, i: (i, h)),
        out_shape=jax.ShapeDtypeStruct((s, n_heads * hw), BF16),
        scratch_shapes=[
            pltpu.VMEM((2, s, hw), BF16),
            pltpu.VMEM((s, 2 * hw), BF16),
            pltpu.VMEM((2, tq, V7X_LANES), F32),
            pltpu.VMEM((2, tq, 2 * hw), F32),
            pltpu.VMEM((2, 2, tq, tk), F32),
            pltpu.VMEM((2, 2, tq, V7X_LANES), F32),
        ],
        compiler_params=_cparams(("arbitrary", "arbitrary"), vmem),
        name="diff_attention",
    )(qkv, qkv, qkv, pos, ones, tri_bias, qc, shift, vec(lq1), vec(lk1), vec(lq2), vec(lk2), vec(subln_w))


def _rg_lru_kernel(rx_ref, rg_ref, cw_ref, cb_ref, wa_ref, ba_ref, wx_ref, bx_ref, lam_ref, o_ref,
                   ext_ref, a_ref, b_ref, tail_ref, h_ref, *, t_rows, cw, blk):
    t = pl.program_id(1)
    halo = V7X_SUBLANES

    @pl.when(t == 0)
    def _():
        tail_ref[...] = jnp.zeros(tail_ref.shape, F32)
        h_ref[...] = jnp.zeros(h_ref.shape, F32)

    rx = rx_ref[...]
    ext_ref[0:halo, :] = tail_ref[...]
    ext_ref[halo:, :] = rx
    tail_ref[...] = rx[t_rows - halo:, :]
    ext = ext_ref[...]
    taps = cw_ref[...]
    n_taps = taps.shape[0]
    xr = cb_ref[...]
    for j in range(n_taps):
        lag = n_taps - 1 - j
        xs = rx if lag == 0 else pltpu.roll(ext, lag, 0)[halo:, :]
        xr = xr + xs * taps[j:j + 1, :]

    ra, ia = [], []
    for g in range(cw // blk):
        xb = xr[:, g * blk:(g + 1) * blk].astype(BF16)
        ra.append(jnp.dot(xb, wa_ref[g], preferred_element_type=F32))
        ia.append(jnp.dot(xb, wx_ref[g], preferred_element_type=F32))
    r = jax.nn.sigmoid(jnp.concatenate(ra, axis=-1) + ba_ref[...])
    ig = jax.nn.sigmoid(jnp.concatenate(ia, axis=-1) + bx_ref[...])
    neg_lam = -lam_ref[...]
    softplus = jnp.maximum(neg_lam, 0.0) + jnp.log1p(jnp.exp(-jnp.abs(neg_lam)))
    log_a = (-RG_C) * r * softplus
    th = jnp.tanh(log_a)
    a_ref[...] = jnp.exp(log_a)
    u = -2.0 * th / (1.0 - th)
    b_ref[...] = jnp.where(u > 0.0, u * lax.rsqrt(u), 0.0) * (ig * xr)

    row = lax.broadcasted_iota(jnp.int32, (V7X_SUBLANES, cw), 0)

    def step(n, h):
        rows = pl.ds(pl.multiple_of(n * V7X_SUBLANES, V7X_SUBLANES), V7X_SUBLANES)
        a = a_ref[rows, :]
        b = b_ref[rows, :]
        d = 1
        while d < V7X_SUBLANES:
            keep = row >= d
            a_s = jnp.where(keep, pltpu.roll(a, d, 0), 1.0)
            b_s = jnp.where(keep, pltpu.roll(b, d, 0), 0.0)
            b = a * b_s + b
            a = a * a_s
            d *= 2
        hh = a * h + b
        b_ref[rows, :] = hh
        return hh[V7X_SUBLANES - 1:, :]

    h_last = lax.fori_loop(0, t_rows // V7X_SUBLANES, step, h_ref[0:1, :], unroll=4)
    h_ref[...] = jnp.broadcast_to(h_last, h_ref.shape)
    o_ref[...] = (b_ref[...] * jax.nn.gelu(rg_ref[...])).astype(o_ref.dtype)


def _rg_lru(rest, conv_w, conv_b, w_a, b_a, w_x, b_x, lru_lambda, d_rnn):
    s = rest.shape[0]
    n_blk, blk, _ = w_a.shape
    cw = _tile(d_rnn, 512, blk)
    t_rows = _tile(s, 1024, BF16_SUBLANES)
    n_cb = d_rnn // cw
    row = lambda a: a.reshape(1, -1).astype(F32)
    kern = functools.partial(_rg_lru_kernel, t_rows=t_rows, cw=cw, blk=blk)
    vec_spec = pl.BlockSpec((1, cw), lambda n, t: (0, n))
    gate_spec = pl.BlockSpec((cw // blk, blk, blk), lambda n, t: (n, 0, 0))
    vmem = 2 * (2 * t_rows * cw * 4 + t_rows * cw * 2) + 12 * t_rows * cw * 4 + 4 * MIB
    return pl.pallas_call(
        kern,
        grid=(n_cb, s // t_rows),
        in_specs=[
            pl.BlockSpec((t_rows, cw), lambda n, t: (t, n)),
            pl.BlockSpec((t_rows, cw), lambda n, t: (t, n_cb + n)),
            pl.BlockSpec((conv_w.shape[0], cw), lambda n, t: (0, n)),
            vec_spec, gate_spec, vec_spec, gate_spec, vec_spec, vec_spec,
        ],
        out_specs=pl.BlockSpec((t_rows, cw), lambda n, t: (t, n)),
        out_shape=jax.ShapeDtypeStruct((s, d_rnn), BF16),
        scratch_shapes=[
            pltpu.VMEM((t_rows + V7X_SUBLANES, cw), F32),
            pltpu.VMEM((t_rows, cw), F32),
            pltpu.VMEM((t_rows, cw), F32),
            pltpu.VMEM((V7X_SUBLANES, cw), F32),
            pltpu.VMEM((V7X_SUBLANES, cw), F32),
        ],
        compiler_params=_cparams(("parallel", "arbitrary"), vmem),
        name="rg_lru",
    )(rest, rest, conv_w.astype(F32), row(conv_b), w_a.astype(BF16), row(b_a), w_x.astype(BF16), row(b_x),
      row(lru_lambda))


def _gated_merge_kernel(ya_ref, yr_ref, wa_ref, wr_ref, ga_ref, gb_ref, o_ref, wab_ref, wrb_ref):
    _cast_weight_once(wa_ref, wab_ref)
    _cast_weight_once(wr_ref, wrb_ref)
    pa = jnp.dot(ya_ref[...], wab_ref[...], preferred_element_type=F32)
    pr = jnp.dot(yr_ref[...], wrb_ref[...], preferred_element_type=F32)
    o_ref[...] = (jax.nn.sigmoid(ga_ref[...]) * pa + jax.nn.sigmoid(gb_ref[...]) * pr).astype(o_ref.dtype)


def _gated_merge(y_att, y_rec, w_pa, w_pr, rest, ga_col, gb_col):
    m, ka = y_att.shape
    _, kr = y_rec.shape
    n = w_pa.shape[1]
    tm = _tile(m, 1024, BF16_SUBLANES)
    tn = _tile(math.gcd(math.gcd(n, ga_col), gb_col), 512, V7X_LANES)
    vmem = (2 * (tm * (ka + kr) * 2 + (ka + kr) * tn * 4 + 2 * tm * tn * 4 + tm * tn * 2) + (ka + kr) * tn * 2
            + 4 * tm * tn * 4 + 2 * MIB)
    return pl.pallas_call(
        _gated_merge_kernel,
        grid=(n // tn, m // tm),
        in_specs=[
            pl.BlockSpec((tm, ka), lambda j, i: (i, 0)),
            pl.BlockSpec((tm, kr), lambda j, i: (i, 0)),
            pl.BlockSpec((ka, tn), lambda j, i: (0, j)),
            pl.BlockSpec((kr, tn), lambda j, i: (0, j)),
            pl.BlockSpec((tm, tn), lambda j, i: (i, ga_col // tn + j)),
            pl.BlockSpec((tm, tn), lambda j, i: (i, gb_col // tn + j)),
        ],
        out_specs=pl.BlockSpec((tm, tn), lambda j, i: (i, j)),
        out_shape=jax.ShapeDtypeStruct((m, n), BF16),
        scratch_shapes=[pltpu.VMEM((ka, tn), BF16), pltpu.VMEM((kr, tn), BF16)],
        compiler_params=_cparams(("arbitrary", "arbitrary"), vmem),
        name="gated_merge",
    )(y_att, y_rec, w_pa, w_pr, rest, rest)


def _matmul_residual_kernel(a_ref, w_ref, x_ref, g_ref, o_ref, wb_ref):
    _cast_weight_once(w_ref, wb_ref)
    y = jnp.dot(a_ref[...], wb_ref[...], preferred_element_type=F32)
    o_ref[...] = x_ref[...] + g_ref[...] * y


def _matmul_residual(a, w, x, mod, gate_idx):
    m, k = a.shape
    n = w.shape[1]
    tm = _tile(m, 1024, BF16_SUBLANES)
    tn = _tile(n, 512, V7X_LANES)
    vmem = 2 * (tm * k * 2 + k * tn * 4 + 2 * tm * tn * 4) + k * tn * 2 + tm * tn * 4 + 2 * MIB
    return pl.pallas_call(
        _matmul_residual_kernel,
        grid=(n // tn, m // tm),
        in_specs=[
            pl.BlockSpec((tm, k), lambda j, i: (i, 0)),
            pl.BlockSpec((k, tn), lambda j, i: (0, j)),
            pl.BlockSpec((tm, tn), lambda j, i: (i, j)),
            pl.BlockSpec((1, tn), lambda j, i: (0, gate_idx * (n // tn) + j)),
        ],
        out_specs=pl.BlockSpec((tm, tn), lambda j, i: (i, j)),
        out_shape=jax.ShapeDtypeStruct((m, n), F32),
        scratch_shapes=[pltpu.VMEM((k, tn), BF16)],
        compiler_params=_cparams(("arbitrary", "arbitrary"), vmem),
        name="matmul_residual",
    )(a, w, x, mod)


def _conv_ffn_kernel(h_ref, wg_ref, wu_ref, cw_ref, cb_ref, wd_ref, x_ref, g_ref, nw_ref, o_ref,
                     gate_ref, up_ref, act_ref, carry_ref, *, tm, dn, rn, rc, nf, final_norm):
    mi = pl.program_id(0)
    f = pl.program_id(1)
    halo = V7X_SUBLANES
    d = o_ref.shape[1]

    def gate_up(slot):
        @pl.when(mi == 0)
        def _():
            carry_ref[f] = jnp.zeros(carry_ref.shape[1:], F32)

        h = h_ref[...]
        gate = jnp.dot(h, wg_ref[...], preferred_element_type=F32)
        up_ref[slot] = jnp.dot(h, wu_ref[...], preferred_element_type=F32)
        gate_ref[slot, 0:halo, :] = carry_ref[f]
        gate_ref[slot, halo:, :] = gate
        carry_ref[f] = gate[tm - halo:, :]

    def activation_rows(slot, r):
        r0 = r * rc
        taps = cw_ref[f - 1]
        n_taps = taps.shape[0]
        gc = cb_ref[f - 1]
        for j in range(n_taps):
            start = halo - (n_taps - 1 - j) + r0
            gc = gc + gate_ref[slot, start:start + rc, :] * taps[j:j + 1, :]
        act_ref[slot, r0:r0 + rc, :] = (jax.nn.gelu(gc) * up_ref[slot, r0:r0 + rc, :]).astype(BF16)

    def down_cols(slot, c):
        cols = slice(c * dn, (c + 1) * dn)
        o_ref[:, cols] += jnp.dot(act_ref[slot], wd_ref[:, cols], preferred_element_type=F32)

    def step(gu_slot, act_slot, down_slot):
        n_c = d // dn
        n_r = tm // rc
        if gu_slot is not None:
            gate_up(gu_slot)
        for k in range(max(n_c, n_r)):
            if act_slot is not None and k < n_r:
                activation_rows(act_slot, k)
            if down_slot is not None and k < n_c:
                down_cols(down_slot, k)

    @pl.when(f == 0)
    def _():
        o_ref[...] = jnp.zeros(o_ref.shape, F32)
        step(0, None, None)

    @pl.when(f == 1)
    def _():
        step(1, 0, None)

    @pl.when(jnp.logical_and(jnp.logical_and(f >= 2, f < nf), f % 2 == 0))
    def _():
        step(0, 1, 0)

    @pl.when(jnp.logical_and(jnp.logical_and(f >= 2, f < nf), f % 2 == 1))
    def _():
        step(1, 0, 1)

    @pl.when(f == nf)
    def _():
        step(None, (nf - 1) % 2, nf % 2)

    @pl.when(f == nf + 1)
    def _():
        step(None, None, (nf - 1) % 2)
        for r in range(tm // rn):
            rows = slice(r * rn, (r + 1) * rn)
            x2 = x_ref[rows, :] + g_ref[...] * o_ref[rows, :]
            if final_norm:
                x2 = x2 * _rms_scale(x2) * nw_ref[...]
            o_ref[rows, :] = x2


def _conv_ffn(h, w_up, conv_w, conv_b, w_down, x, mod, gate_idx, norm_w, final_norm):
    s, d = h.shape
    d_ff = w_down.shape[0]
    tm = _tile(s, 512, BF16_SUBLANES)
    tf = _tile(d_ff, 256, V7X_LANES)
    nf = d_ff // tf
    assert nf >= 2
    dn = _tile(d, 512, V7X_LANES)
    rn = _tile(tm, 128, V7X_SUBLANES)
    rc = _tile(tm, 64, BF16_SUBLANES)
    kern = functools.partial(_conv_ffn_kernel, tm=tm, dn=dn, rn=rn, rc=rc, nf=nf, final_norm=final_norm)
    vmem = (2 * (tm * d * 2 + 2 * d * tf * 2 + tf * d * 2 + tm * d * 4) + tm * d * 4
            + 12 * tm * tf * 4 + 4 * rn * d * 4 + tm * dn * 4 + 2 * MIB)
    chunk = lambda f, lag: jnp.clip(f - lag, 0, nf - 1)
    n_taps = conv_w.shape[0]
    taps_by_chunk = conv_w.astype(F32).reshape(n_taps, nf, tf).transpose(1, 0, 2)
    bias_by_chunk = conv_b.astype(F32).reshape(nf, 1, tf)
    return pl.pallas_call(
        kern,
        grid=(s // tm, nf + 2),
        in_specs=[
            pl.BlockSpec((tm, d), lambda i, f: (i, 0)),
            pl.BlockSpec((d, tf), lambda i, f: (0, chunk(f, 0))),
            pl.BlockSpec((d, tf), lambda i, f: (0, nf + chunk(f, 0))),
            pl.BlockSpec((nf, n_taps, tf), lambda i, f: (0, 0, 0)),
            pl.BlockSpec((nf, 1, tf), lambda i, f: (0, 0, 0)),
            pl.BlockSpec((tf, d), lambda i, f: (chunk(f, 2), 0)),
            pl.BlockSpec((tm, d), lambda i, f: (i, 0), pipeline_mode=pl.Buffered(1)),
            pl.BlockSpec((1, d), lambda i, f: (0, gate_idx)),
            pl.BlockSpec((1, d), lambda i, f: (0, 0)),
        ],
        out_specs=pl.BlockSpec((tm, d), lambda i, f: (i, 0)),
        out_shape=jax.ShapeDtypeStruct((s, d), F32),
        scratch_shapes=[
            pltpu.VMEM((2, tm + V7X_SUBLANES, tf), F32),
            pltpu.VMEM((2, tm, tf), F32),
            pltpu.VMEM((2, tm, tf), BF16),
            pltpu.VMEM((nf, V7X_SUBLANES, tf), F32),
        ],
        compiler_params=_cparams(("arbitrary", "arbitrary"), vmem),
        name="conv_ffn",
    )(h, w_up, w_up, taps_by_chunk, bias_by_chunk, w_down, x, mod, norm_w.reshape(1, -1).astype(F32))


def kernel(x, c, w_ada, b_ada, norm1_w, w_in, lambda_q1, lambda_k1, lambda_q2, lambda_k2, subln_w, conv_w, conv_b, w_rg_a, b_rg_a, w_rg_x, b_rg_x, lru_lambda, w_proj_attn, w_proj_rec, w_out, norm2_w, w_ffn_up, ffn_conv_w, ffn_conv_b, w_ffn_down, norm_f_w):
    batch, seq, d = x.shape
    depth = w_ada.shape[0]
    dh = lambda_q1.shape[-1]
    attn_w = w_proj_attn.shape[1]
    n_heads = attn_w // (2 * dh)
    d_rnn = conv_w.shape[-1]
    assert w_in.shape[-1] == 3 * attn_w + 2 * d_rnn + 2 * d
    row = lambda a: a.reshape(1, -1).astype(F32)

    outs = []
    for b in range(batch):
        xb = x[b]
        cb = c[b:b + 1]
        for l in range(depth):
            lambda_init = 0.8 - 0.6 * math.exp(-0.3 * l)
            mod = _adaln_mod(cb, w_ada[l], row(b_ada[l]))
            h1 = _norm_modulate(xb, row(norm1_w[l]), mod, 0, 1)
            q_scale = jnp.concatenate([jnp.full((1, attn_w), dh ** -0.5 * LOG2E, F32),
                                       jnp.ones((1, 2 * attn_w), F32)], axis=1)
            qkv = _matmul(h1, w_in[l], 0, 3 * attn_w, BF16, q_scale)
            rest = _matmul(h1, w_in[l], 3 * attn_w, 2 * d_rnn + 2 * d, F32)
            y_att = _diff_attention(qkv, lambda_q1[l], lambda_k1[l], lambda_q2[l], lambda_k2[l], subln_w[l],
                                    lambda_init, n_heads, dh)
            y_rec = _rg_lru(rest, conv_w[l], conv_b[l], w_rg_a[l], b_rg_a[l], w_rg_x[l], b_rg_x[l],
                            lru_lambda[l], d_rnn)
            mixed = _gated_merge(y_att, y_rec, w_proj_attn[l], w_proj_rec[l], rest, 2 * d_rnn, 2 * d_rnn + d)
            x1 = _matmul_residual(mixed, w_out[l], xb, mod, 2)
            h2 = _norm_modulate(x1, row(norm2_w[l]), mod, 3, 4)
            last = l == depth - 1
            xb = _conv_ffn(h2, w_ffn_up[l].astype(BF16), ffn_conv_w[l], ffn_conv_b[l], w_ffn_down[l].astype(BF16),
                           x1, mod, 5, norm_f_w, final_norm=last)
        outs.append(xb)
    return outs[0][None] if batch == 1 else jnp.stack(outs, axis=0)
```

```python
import functools
import math

import jax
import jax.numpy as jnp
from jax import lax
from jax.experimental import pallas as pl
from jax.experimental.pallas import tpu as pltpu

F32 = jnp.float32
BF16 = jnp.bfloat16
EPS = 1e-6
RG_C = 8.0
NEG_BIG = -1e30

V7X_LANES = 128
V7X_SUBLANES = 8
BF16_SUBLANES = 16
V7X_VMEM_LIMIT_BYTES = 60000 * 1024
MIB = 1024 * 1024


def _cparams(semantics, vmem_bytes):
    return pltpu.CompilerParams(
        dimension_semantics=semantics,
        vmem_limit_bytes=int(min(vmem_bytes, V7X_VMEM_LIMIT_BYTES)),
    )


def _tile(n, want, quantum):
    t = min(n, want) // quantum * quantum
    while t > quantum and n % t:
        t -= quantum
    assert t >= quantum and n % t == 0, (n, want, quantum)
    return t


def _rms_scale(x):
    return lax.rsqrt(jnp.mean(x * x, axis=-1, keepdims=True) + EPS)


def _adaln_mod_kernel(c_ref, w_ref, b_ref, o_ref):
    c = c_ref[...]
    a = c * jax.nn.sigmoid(c)
    a8 = jnp.broadcast_to(a, (V7X_SUBLANES, a.shape[1])).astype(BF16)
    acc = jnp.dot(a8, w_ref[...].astype(BF16), preferred_element_type=F32)
    o_ref[...] = acc[0:1, :] + b_ref[...]


def _adaln_mod(c, w, b):
    d, n = w.shape
    tn = _tile(n, 1024, V7X_LANES)
    return pl.pallas_call(
        _adaln_mod_kernel,
        grid=(n // tn,),
        in_specs=[
            pl.BlockSpec((1, d), lambda j: (0, 0)),
            pl.BlockSpec((d, tn), lambda j: (0, j)),
            pl.BlockSpec((1, tn), lambda j: (0, j)),
        ],
        out_specs=pl.BlockSpec((1, tn), lambda j: (0, j)),
        out_shape=jax.ShapeDtypeStruct((1, n), F32),
        compiler_params=_cparams(("arbitrary",), 2 * d * tn * 4 + d * tn * 2 + 4 * MIB),
        name="adaln_mod",
    )(c, w, b)


def _norm_modulate_kernel(x_ref, w_ref, scale_ref, shift_ref, o_ref):
    x = x_ref[...]
    y = x * _rms_scale(x) * w_ref[...]
    o_ref[...] = (y * (1.0 + scale_ref[...]) + shift_ref[...]).astype(o_ref.dtype)


def _norm_modulate(x, w, mod, shift_idx, scale_idx):
    s, d = x.shape
    tm = _tile(s, 512, BF16_SUBLANES)
    return pl.pallas_call(
        _norm_modulate_kernel,
        grid=(s // tm,),
        in_specs=[
            pl.BlockSpec((tm, d), lambda i: (i, 0)),
            pl.BlockSpec((1, d), lambda i: (0, 0)),
            pl.BlockSpec((1, d), lambda i: (0, scale_idx)),
            pl.BlockSpec((1, d), lambda i: (0, shift_idx)),
        ],
        out_specs=pl.BlockSpec((tm, d), lambda i: (i, 0)),
        out_shape=jax.ShapeDtypeStruct((s, d), BF16),
        compiler_params=_cparams(("parallel",), 2 * tm * d * 6 + 3 * tm * d * 4 + 2 * MIB),
        name="norm_modulate",
    )(x, w, mod, mod)


def _cast_weight_once(w_ref, wb_ref):
    @pl.when(pl.program_id(1) == 0)
    def _():
        wb_ref[...] = w_ref[...].astype(BF16)


def _matmul_kernel(x_ref, w_ref, o_ref, wb_ref):
    _cast_weight_once(w_ref, wb_ref)
    o_ref[...] = jnp.dot(x_ref[...], wb_ref[...], preferred_element_type=F32).astype(o_ref.dtype)


def _matmul_colscale_kernel(x_ref, w_ref, s_ref, o_ref, wb_ref):
    _cast_weight_once(w_ref, wb_ref)
    y = jnp.dot(x_ref[...], wb_ref[...], preferred_element_type=F32)
    o_ref[...] = (y * s_ref[...]).astype(o_ref.dtype)


def _matmul(x, w, col0, n, out_dtype, col_scale=None):
    m, k = x.shape
    tm = _tile(m, 1024, BF16_SUBLANES)
    tn = _tile(math.gcd(n, col0) if col0 else n, 512, V7X_LANES)
    ob = jnp.dtype(out_dtype).itemsize
    vmem = 2 * (tm * k * 2 + k * tn * 4 + tm * tn * ob) + k * tn * 2 + 2 * tm * tn * 4 + 2 * MIB
    in_specs = [
        pl.BlockSpec((tm, k), lambda j, i: (i, 0)),
        pl.BlockSpec((k, tn), lambda j, i: (0, col0 // tn + j)),
    ]
    args = (x, w)
    if col_scale is not None:
        in_specs.append(pl.BlockSpec((1, tn), lambda j, i: (0, j)))
        args = (x, w, col_scale)
    return pl.pallas_call(
        _matmul_kernel if col_scale is None else _matmul_colscale_kernel,
        grid=(n // tn, m // tm),
        in_specs=in_specs,
        out_specs=pl.BlockSpec((tm, tn), lambda j, i: (i, j)),
        out_shape=jax.ShapeDtypeStruct((m, n), out_dtype),
        scratch_shapes=[pltpu.VMEM((k, tn), BF16)],
        compiler_params=_cparams(("arbitrary", "arbitrary"), vmem),
        name="matmul",
    )(*args)


LOG2E = math.log2(math.e)
POS_RADIX = 32
N_BIAS_COLS = 6


def _split3_bf16(x):
    hi = x.astype(BF16).astype(F32)
    mid = (x - hi).astype(BF16).astype(F32)
    lo = (x - hi - mid).astype(BF16).astype(F32)
    return hi, mid, lo


def _diff_attn_kernel(q_ref, k_ref, v_ref, pos_ref, ones_ref, tri_ref, qc_ref, sh_ref, lq1_ref, lk1_ref, lq2_ref, lk2_ref,
                      sw_ref, o_ref, ka_ref, va_ref, m_ref, acc_ref, z_ref, zmax_ref, *, tq, tk, dh, lambda_init):
    i = pl.program_id(1)
    hw = 2 * dh
    s_len = k_ref.shape[0]

    @pl.when(i == 0)
    def _():
        lane_k = lax.broadcasted_iota(jnp.int32, (tk, hw), 1)

        def prep(j, carry):
            rows = pl.ds(pl.multiple_of(j * tk, tk), tk)
            kj = k_ref[rows, :]
            ka_ref[0, rows, :] = jnp.where(lane_k < dh, kj, pos_ref[0])
            ka_ref[1, rows, :] = jnp.where(lane_k >= dh, kj, pos_ref[1])
            va_ref[rows, 0:hw] = v_ref[rows, :]
            va_ref[rows, hw:2 * hw] = ones_ref[...]
            return carry

        lax.fori_loop(0, s_len // tk, prep, 0)

    lane_q = lax.broadcasted_iota(jnp.int32, (tq, hw), 1)
    qc = qc_ref[0].astype(BF16)

    def score_operands(q):
        return (jnp.where(lane_q < dh, q, qc[0:1, :]), jnp.where(lane_q >= dh, q, qc[1:2, :]))

    q_maps = score_operands(q_ref[...])
    frame_shift = sh_ref[0][:, 0:1]

    m_ref[...] = jnp.full(m_ref.shape, NEG_BIG, F32)
    acc_ref[...] = jnp.zeros(acc_ref.shape, F32)

    def scores(maps, j, c):
        rows = pl.ds(pl.multiple_of(j * tk, tk), tk)
        return lax.dot_general(maps[c], ka_ref[c, rows, :], (((1,), (1,)), ((), ())),
                               preferred_element_type=F32)

    def lanes(x, width):
        return jnp.concatenate([x] * (width // V7X_LANES), axis=-1)

    def row_max(z):
        return jnp.broadcast_to(jnp.max(z, axis=-1, keepdims=True), (tq, V7X_LANES))

    def update(j, c, z, z_max):
        width = z.shape[1]
        rows = pl.ds(pl.multiple_of(j * tk, tk), width)
        m_prev = m_ref[c] - frame_shift
        m_new = jnp.maximum(m_prev, z_max)
        alpha = jnp.exp2(m_prev - m_new)
        p = jnp.exp2(z - lanes(m_new, width)).astype(BF16)
        acc_ref[c] = lanes(alpha, 2 * hw) * acc_ref[c] + jnp.dot(p, va_ref[rows, :], preferred_element_type=F32)
        m_ref[c] = m_new

    n_full = (i * tq) // tk

    def issue_scores(j, slot):
        for c in range(2):
            z = scores(q_maps, j, c)
            z_ref[slot, c] = z
            zmax_ref[slot, c] = row_max(z)

    def step(j, cur):
        issue_scores(j + 1, 1 - cur)
        for c in range(2):
            update(j, c, z_ref[cur, c], zmax_ref[cur, c])

    def diagonal_step(cur):
        for pos in range(tk // tq):
            @pl.when(i % (tk // tq) == pos)
            def _():
                lo, width = pos * tq, (pos + 1) * tq
                for c in range(2):
                    diag = z_ref[cur, c, :, lo:width] + tri_ref[...]
                    z = diag if pos == 0 else jnp.concatenate([z_ref[cur, c, :, 0:lo], diag], axis=-1)
                    update(n_full, c, z, row_max(z))

    issue_scores(0, 0)

    def chunk_pair(t, carry):
        step(2 * t, 0)
        step(2 * t + 1, 1)
        return carry

    lax.fori_loop(0, n_full // 2, chunk_pair, 0)

    @pl.when(n_full % 2 == 1)
    def _():
        step(n_full - 1, 0)
        diagonal_step(1)

    @pl.when(n_full % 2 == 0)
    def _():
        diagonal_step(0)

    a0 = acc_ref[0]
    a1 = acc_ref[1]
    o1 = a0[:, 0:hw] / a0[:, hw:2 * hw]
    o2 = a1[:, 0:hw] / a1[:, hw:2 * hw]
    lam = (jnp.exp(jnp.sum(lq1_ref[...] * lk1_ref[...], axis=-1, keepdims=True))
           - jnp.exp(jnp.sum(lq2_ref[...] * lk2_ref[...], axis=-1, keepdims=True)) + lambda_init)
    o = o1 - lam * o2
    y = o * _rms_scale(o) * sw_ref[...]
    o_ref[...] = (y * (1.0 - lambda_init)).astype(o_ref.dtype)


def _diff_attention(qkv, lq1, lk1, lq2, lk2, subln_w, lambda_init, n_heads, dh):
    s = qkv.shape[0]
    hw = 2 * dh
    tk = _tile(s, 1024, V7X_LANES)
    tq = _tile(tk, 512, BF16_SUBLANES)
    nb = N_BIAS_COLS
    assert dh >= nb and tk // POS_RADIX <= 256
    slopes2 = (2.0 ** (-8.0 * jnp.arange(1, n_heads + 1, dtype=F32) / n_heads)) * LOG2E
    pieces = jnp.stack(_split3_bf16(slopes2), axis=-1)
    q_cols = jnp.concatenate([pieces * POS_RADIX, pieces], axis=-1)
    qc = jnp.zeros((n_heads, 2, hw), F32).at[:, 0, dh:dh + nb].set(q_cols).at[:, 1, 0:nb].set(q_cols)
    jj = jnp.arange(tk, dtype=jnp.int32)
    k_cols = jnp.stack([jj // POS_RADIX] * 3 + [jj % POS_RADIX] * 3, axis=-1).astype(F32)
    pos = jnp.zeros((2, tk, hw), F32).at[0, :, dh:dh + nb].set(k_cols).at[1, :, 0:nb].set(k_cols).astype(BF16)
    ones = jnp.ones((tk, hw), BF16)
    shift = jnp.broadcast_to((slopes2 * tk)[:, None, None], (n_heads, 1, V7X_LANES))
    rel = jnp.arange(tq, dtype=jnp.int32)[None, :] - jnp.arange(tq, dtype=jnp.int32)[:, None]
    tri_bias = jnp.where(rel <= 0, 0.0, NEG_BIG).astype(F32)
    kern = functools.partial(_diff_attn_kernel, tq=tq, tk=tk, dh=dh, lambda_init=lambda_init)
    vec = lambda a: a.reshape(1, -1).astype(F32)
    const2 = lambda h, i: (0, 0)
    vmem = (2 * (2 * s * hw * 2 + 2 * tq * hw * 2 + 3 * tk * hw * 2) + 4 * s * hw * 2
            + 8 * tq * tk * 4 + 2 * tq * (2 * hw + V7X_LANES) * 4 + 4 * MIB)
    return pl.pallas_call(
        kern,
        grid=(n_heads, s // tq),
        in_specs=[
            pl.BlockSpec((tq, hw), lambda h, i: (i, h)),
            pl.BlockSpec((s, hw), lambda h, i: (0, n_heads + h)),
            pl.BlockSpec((s, hw), lambda h, i: (0, 2 * n_heads + h)),
            pl.BlockSpec((2, tk, hw), lambda h, i: (0, 0, 0)),
            pl.BlockSpec((tk, hw), const2),
            pl.BlockSpec((tq, tq), const2),
            pl.BlockSpec((1, 2, hw), lambda h, i: (h, 0, 0)),
            pl.BlockSpec((1, 1, V7X_LANES), lambda h, i: (h, 0, 0)),
            pl.BlockSpec((1, dh), const2),
            pl.BlockSpec((1, dh), const2),
            pl.BlockSpec((1, dh), const2),
            pl.BlockSpec((1, dh), const2),
            pl.BlockSpec((1, hw), const2),
        ],
        out_specs=pl.BlockSpec((tq, hw), lambda h, i: (i, h)),
        out_shape=jax.ShapeDtypeStruct((s, n_heads * hw), BF16),
        scratch_shapes=[
            pltpu.VMEM((2, s, hw), BF16),
            pltpu.VMEM((s, 2 * hw), BF16),
            pltpu.VMEM((2, tq, V7X_LANES), F32),
            pltpu.VMEM((2, tq, 2 * hw), F32),
            pltpu.VMEM((2, 2, tq, tk), F32),
            pltpu.VMEM((2, 2, tq, V7X_LANES), F32),
        ],
        compiler_params=_cparams(("arbitrary", "arbitrary"), vmem),
        name="diff_attention",
    )(qkv, qkv, qkv, pos, ones, tri_bias, qc, shift, vec(lq1), vec(lk1), vec(lq2), vec(lk2), vec(subln_w))


def _rg_lru_kernel(rx_ref, rg_ref, cw_ref, cb_ref, wa_ref, ba_ref, wx_ref, bx_ref, lam_ref, o_ref,
                   ext_ref, a_ref, b_ref, tail_ref, h_ref, *, t_rows, cw, blk):
    t = pl.program_id(1)
    halo = V7X_SUBLANES

    @pl.when(t == 0)
    def _():
        tail_ref[...] = jnp.zeros(tail_ref.shape, F32)
        h_ref[...] = jnp.zeros(h_ref.shape, F32)

    rx = rx_ref[...]
    ext_ref[0:halo, :] = tail_ref[...]
    ext_ref[halo:, :] = rx
    tail_ref[...] = rx[t_rows - halo:, :]
    ext = ext_ref[...]
    taps = cw_ref[...]
    n_taps = taps.shape[0]
    xr = cb_ref[...]
    for j in range(n_taps):
        lag = n_taps - 1 - j
        xs = rx if lag == 0 else pltpu.roll(ext, lag, 0)[halo:, :]
        xr = xr + xs * taps[j:j + 1, :]

    ra, ia = [], []
    for g in range(cw // blk):
        xb = xr[:, g * blk:(g + 1) * blk].astype(BF16)
        ra.append(jnp.dot(xb, wa_ref[g], preferred_element_type=F32))
        ia.append(jnp.dot(xb, wx_ref[g], preferred_element_type=F32))
    r = jax.nn.sigmoid(jnp.concatenate(ra, axis=-1) + ba_ref[...])
    ig = jax.nn.sigmoid(jnp.concatenate(ia, axis=-1) + bx_ref[...])
    neg_lam = -lam_ref[...]
    softplus = jnp.maximum(neg_lam, 0.0) + jnp.log1p(jnp.exp(-jnp.abs(neg_lam)))
    log_a = (-RG_C) * r * softplus
    th = jnp.tanh(log_a)
    a_ref[...] = jnp.exp(log_a)
    u = -2.0 * th / (1.0 - th)
    b_ref[...] = jnp.where(u > 0.0, u * lax.rsqrt(u), 0.0) * (ig * xr)

    row = lax.broadcasted_iota(jnp.int32, (V7X_SUBLANES, cw), 0)

    def step(n, h):
        rows = pl.ds(pl.multiple_of(n * V7X_SUBLANES, V7X_SUBLANES), V7X_SUBLANES)
        a = a_ref[rows, :]
        b = b_ref[rows, :]
        d = 1
        while d < V7X_SUBLANES:
            keep = row >= d
            a_s = jnp.where(keep, pltpu.roll(a, d, 0), 1.0)
            b_s = jnp.where(keep, pltpu.roll(b, d, 0), 0.0)
            b = a * b_s + b
            a = a * a_s
            d *= 2
        hh = a * h + b
        b_ref[rows, :] = hh
        return hh[V7X_SUBLANES - 1:, :]

    h_last = lax.fori_loop(0, t_rows // V7X_SUBLANES, step, h_ref[0:1, :], unroll=4)
    h_ref[...] = jnp.broadcast_to(h_last, h_ref.shape)
    o_ref[...] = (b_ref[...] * jax.nn.gelu(rg_ref[...])).astype(o_ref.dtype)


def _rg_lru(rest, conv_w, conv_b, w_a, b_a, w_x, b_x, lru_lambda, d_rnn):
    s = rest.shape[0]
    n_blk, blk, _ = w_a.shape
    cw = _tile(d_rnn, 512, blk)
    t_rows = _tile(s, 2048, BF16_SUBLANES)
    n_cb = d_rnn // cw
    row = lambda a: a.reshape(1, -1).astype(F32)
    kern = functools.partial(_rg_lru_kernel, t_rows=t_rows, cw=cw, blk=blk)
    vec_spec = pl.BlockSpec((1, cw), lambda n, t: (0, n))
    gate_spec = pl.BlockSpec((cw // blk, blk, blk), lambda n, t: (n, 0, 0))
    vmem = 2 * (2 * t_rows * cw * 4 + t_rows * cw * 2) + 12 * t_rows * cw * 4 + 4 * MIB
    return pl.pallas_call(
        kern,
        grid=(n_cb, s // t_rows),
        in_specs=[
            pl.BlockSpec((t_rows, cw), lambda n, t: (t, n)),
            pl.BlockSpec((t_rows, cw), lambda n, t: (t, n_cb + n)),
            pl.BlockSpec((conv_w.shape[0], cw), lambda n, t: (0, n)),
            vec_spec, gate_spec, vec_spec, gate_spec, vec_spec, vec_spec,
        ],
        out_specs=pl.BlockSpec((t_rows, cw), lambda n, t: (t, n)),
        out_shape=jax.ShapeDtypeStruct((s, d_rnn), BF16),
        scratch_shapes=[
            pltpu.VMEM((t_rows + V7X_SUBLANES, cw), F32),
            pltpu.VMEM((t_rows, cw), F32),
            pltpu.VMEM((t_rows, cw), F32),
            pltpu.VMEM((V7X_SUBLANES, cw), F32),
            pltpu.VMEM((V7X_SUBLANES, cw), F32),
        ],
        compiler_params=_cparams(("parallel", "arbitrary"), vmem),
        name="rg_lru",
    )(rest, rest, conv_w.astype(F32), row(conv_b), w_a.astype(BF16), row(b_a), w_x.astype(BF16), row(b_x),
      row(lru_lambda))


def _gated_merge_kernel(ya_ref, yr_ref, wa_ref, wr_ref, ga_ref, gb_ref, o_ref, wab_ref, wrb_ref):
    _cast_weight_once(wa_ref, wab_ref)
    _cast_weight_once(wr_ref, wrb_ref)
    pa = jnp.dot(ya_ref[...], wab_ref[...], preferred_element_type=F32)
    pr = jnp.dot(yr_ref[...], wrb_ref[...], preferred_element_type=F32)
    o_ref[...] = (jax.nn.sigmoid(ga_ref[...]) * pa + jax.nn.sigmoid(gb_ref[...]) * pr).astype(o_ref.dtype)


def _gated_merge(y_att, y_rec, w_pa, w_pr, rest, ga_col, gb_col):
    m, ka = y_att.shape
    _, kr = y_rec.shape
    n = w_pa.shape[1]
    tm = _tile(m, 1024, BF16_SUBLANES)
    tn = _tile(math.gcd(math.gcd(n, ga_col), gb_col), 512, V7X_LANES)
    vmem = (2 * (tm * (ka + kr) * 2 + (ka + kr) * tn * 4 + 2 * tm * tn * 4 + tm * tn * 2) + (ka + kr) * tn * 2
            + 4 * tm * tn * 4 + 2 * MIB)
    return pl.pallas_call(
        _gated_merge_kernel,
        grid=(n // tn, m // tm),
        in_specs=[
            pl.BlockSpec((tm, ka), lambda j, i: (i, 0)),
            pl.BlockSpec((tm, kr), lambda j, i: (i, 0)),
            pl.BlockSpec((ka, tn), lambda j, i: (0, j)),
            pl.BlockSpec((kr, tn), lambda j, i: (0, j)),
            pl.BlockSpec((tm, tn), lambda j, i: (i, ga_col // tn + j)),
            pl.BlockSpec((tm, tn), lambda j, i: (i, gb_col // tn + j)),
        ],
        out_specs=pl.BlockSpec((tm, tn), lambda j, i: (i, j)),
        out_shape=jax.ShapeDtypeStruct((m, n), BF16),
        scratch_shapes=[pltpu.VMEM((ka, tn), BF16), pltpu.VMEM((kr, tn), BF16)],
        compiler_params=_cparams(("arbitrary", "arbitrary"), vmem),
        name="gated_merge",
    )(y_att, y_rec, w_pa, w_pr, rest, rest)


def _matmul_residual_kernel(a_ref, w_ref, x_ref, g_ref, o_ref, wb_ref):
    _cast_weight_once(w_ref, wb_ref)
    y = jnp.dot(a_ref[...], wb_ref[...], preferred_element_type=F32)
    o_ref[...] = x_ref[...] + g_ref[...] * y


def _matmul_residual(a, w, x, mod, gate_idx):
    m, k = a.shape
    n = w.shape[1]
    tm = _tile(m, 1024, BF16_SUBLANES)
    tn = _tile(n, 512, V7X_LANES)
    vmem = 2 * (tm * k * 2 + k * tn * 4 + 2 * tm * tn * 4) + k * tn * 2 + tm * tn * 4 + 2 * MIB
    return pl.pallas_call(
        _matmul_residual_kernel,
        grid=(n // tn, m // tm),
        in_specs=[
            pl.BlockSpec((tm, k), lambda j, i: (i, 0)),
            pl.BlockSpec((k, tn), lambda j, i: (0, j)),
            pl.BlockSpec((tm, tn), lambda j, i: (i, j)),
            pl.BlockSpec((1, tn), lambda j, i: (0, gate_idx * (n // tn) + j)),
        ],
        out_specs=pl.BlockSpec((tm, tn), lambda j, i: (i, j)),
        out_shape=jax.ShapeDtypeStruct((m, n), F32),
        scratch_shapes=[pltpu.VMEM((k, tn), BF16)],
        compiler_params=_cparams(("arbitrary", "arbitrary"), vmem),
        name="matmul_residual",
    )(a, w, x, mod)


def _conv_ffn_kernel(h_ref, wg_ref, wu_ref, cw_ref, cb_ref, wd_ref, x_ref, g_ref, nw_ref, o_ref,
                     gate_ref, up_ref, act_ref, carry_ref, *, tm, dn, rn, rc, nf, final_norm):
    mi = pl.program_id(0)
    f = pl.program_id(1)
    halo = V7X_SUBLANES
    d = o_ref.shape[1]

    def gate_up(slot):
        @pl.when(mi == 0)
        def _():
            carry_ref[f] = jnp.zeros(carry_ref.shape[1:], F32)

        h = h_ref[...]
        gate = jnp.dot(h, wg_ref[...], preferred_element_type=F32)
        up_ref[slot] = jnp.dot(h, wu_ref[...], preferred_element_type=F32)
        gate_ref[slot, 0:halo, :] = carry_ref[f]
        gate_ref[slot, halo:, :] = gate
        carry_ref[f] = gate[tm - halo:, :]

    def activation_rows(slot, r):
        r0 = r * rc
        taps = cw_ref[f - 1]
        n_taps = taps.shape[0]
        gc = cb_ref[f - 1]
        for j in range(n_taps):
            start = halo - (n_taps - 1 - j) + r0
            gc = gc + gate_ref[slot, start:start + rc, :] * taps[j:j + 1, :]
        act_ref[slot, r0:r0 + rc, :] = (jax.nn.gelu(gc) * up_ref[slot, r0:r0 + rc, :]).astype(BF16)

    def down_cols(slot, c):
        cols = slice(c * dn, (c + 1) * dn)
        o_ref[:, cols] += jnp.dot(act_ref[slot], wd_ref[:, cols], preferred_element_type=F32)

    def step(gu_slot, act_slot, down_slot):
        n_c = d // dn
        n_r = tm // rc
        if gu_slot is not None:
            gate_up(gu_slot)
        for k in range(max(n_c, n_r)):
            if act_slot is not None and k < n_r:
                activation_rows(act_slot, k)
            if down_slot is not None and k < n_c:
                down_cols(down_slot, k)

    @pl.when(f == 0)
    def _():
        o_ref[...] = jnp.zeros(o_ref.shape, F32)
        step(0, None, None)

    @pl.when(f == 1)
    def _():
        step(1, 0, None)

    @pl.when(jnp.logical_and(jnp.logical_and(f >= 2, f < nf), f % 2 == 0))
    def _():
        step(0, 1, 0)

    @pl.when(jnp.logical_and(jnp.logical_and(f >= 2, f < nf), f % 2 == 1))
    def _():
        step(1, 0, 1)

    @pl.when(f == nf)
    def _():
        step(None, (nf - 1) % 2, nf % 2)

    @pl.when(f == nf + 1)
    def _():
        step(None, None, (nf - 1) % 2)
        for r in range(tm // rn):
            rows = slice(r * rn, (r + 1) * rn)
            x2 = x_ref[rows, :] + g_ref[...] * o_ref[rows, :]
            if final_norm:
                x2 = x2 * _rms_scale(x2) * nw_ref[...]
            o_ref[rows, :] = x2


def _conv_ffn(h, w_up, conv_w, conv_b, w_down, x, mod, gate_idx, norm_w, final_norm):
    s, d = h.shape
    d_ff = w_down.shape[0]
    tm = _tile(s, 512, BF16_SUBLANES)
    tf = _tile(d_ff, 256, V7X_LANES)
    nf = d_ff // tf
    assert nf >= 2
    dn = _tile(d, 512, V7X_LANES)
    rn = _tile(tm, 128, V7X_SUBLANES)
    rc = _tile(tm, 64, BF16_SUBLANES)
    kern = functools.partial(_conv_ffn_kernel, tm=tm, dn=dn, rn=rn, rc=rc, nf=nf, final_norm=final_norm)
    vmem = (2 * (tm * d * 2 + 2 * d * tf * 2 + tf * d * 2 + tm * d * 4) + tm * d * 4
            + 12 * tm * tf * 4 + 4 * rn * d * 4 + tm * dn * 4 + 2 * MIB)
    chunk = lambda f, lag: jnp.clip(f - lag, 0, nf - 1)
    n_taps = conv_w.shape[0]
    taps_by_chunk = conv_w.astype(F32).reshape(n_taps, nf, tf).transpose(1, 0, 2)
    bias_by_chunk = conv_b.astype(F32).reshape(nf, 1, tf)
    return pl.pallas_call(
        kern,
        grid=(s // tm, nf + 2),
        in_specs=[
            pl.BlockSpec((tm, d), lambda i, f: (i, 0)),
            pl.BlockSpec((d, tf), lambda i, f: (0, chunk(f, 0))),
            pl.BlockSpec((d, tf), lambda i, f: (0, nf + chunk(f, 0))),
            pl.BlockSpec((nf, n_taps, tf), lambda i, f: (0, 0, 0)),
            pl.BlockSpec((nf, 1, tf), lambda i, f: (0, 0, 0)),
            pl.BlockSpec((tf, d), lambda i, f: (chunk(f, 2), 0)),
            pl.BlockSpec((tm, d), lambda i, f: (i, 0), pipeline_mode=pl.Buffered(1)),
            pl.BlockSpec((1, d), lambda i, f: (0, gate_idx)),
            pl.BlockSpec((1, d), lambda i, f: (0, 0)),
        ],
        out_specs=pl.BlockSpec((tm, d), lambda i, f: (i, 0)),
        out_shape=jax.ShapeDtypeStruct((s, d), F32),
        scratch_shapes=[
            pltpu.VMEM((2, tm + V7X_SUBLANES, tf), F32),
            pltpu.VMEM((2, tm, tf), F32),
            pltpu.VMEM((2, tm, tf), BF16),
            pltpu.VMEM((nf, V7X_SUBLANES, tf), F32),
        ],
        compiler_params=_cparams(("arbitrary", "arbitrary"), vmem),
        name="conv_ffn",
    )(h, w_up, w_up, taps_by_chunk, bias_by_chunk, w_down, x, mod, norm_w.reshape(1, -1).astype(F32))


def kernel(x, c, w_ada, b_ada, norm1_w, w_in, lambda_q1, lambda_k1, lambda_q2, lambda_k2, subln_w, conv_w, conv_b, w_rg_a, b_rg_a, w_rg_x, b_rg_x, lru_lambda, w_proj_attn, w_proj_rec, w_out, norm2_w, w_ffn_up, ffn_conv_w, ffn_conv_b, w_ffn_down, norm_f_w):
    batch, seq, d = x.shape
    depth = w_ada.shape[0]
    dh = lambda_q1.shape[-1]
    attn_w = w_proj_attn.shape[1]
    n_heads = attn_w // (2 * dh)
    d_rnn = conv_w.shape[-1]
    assert w_in.shape[-1] == 3 * attn_w + 2 * d_rnn + 2 * d
    row = lambda a: a.reshape(1, -1).astype(F32)

    outs = []
    for b in range(batch):
        xb = x[b]
        cb = c[b:b + 1]
        for l in range(depth):
            lambda_init = 0.8 - 0.6 * math.exp(-0.3 * l)
            mod = _adaln_mod(cb, w_ada[l], row(b_ada[l]))
            h1 = _norm_modulate(xb, row(norm1_w[l]), mod, 0, 1)
            q_scale = jnp.concatenate([jnp.full((1, attn_w), dh ** -0.5 * LOG2E, F32),
                                       jnp.ones((1, 2 * attn_w), F32)], axis=1)
            qkv = _matmul(h1, w_in[l], 0, 3 * attn_w, BF16, q_scale)
            rest = _matmul(h1, w_in[l], 3 * attn_w, 2 * d_rnn + 2 * d, F32)
            y_att = _diff_attention(qkv, lambda_q1[l], lambda_k1[l], lambda_q2[l], lambda_k2[l], subln_w[l],
                                    lambda_init, n_heads, dh)
            y_rec = _rg_lru(rest, conv_w[l], conv_b[l], w_rg_a[l], b_rg_a[l], w_rg_x[l], b_rg_x[l],
                            lru_lambda[l], d_rnn)
            mixed = _gated_merge(y_att, y_rec, w_proj_attn[l], w_proj_rec[l], rest, 2 * d_rnn, 2 * d_rnn + d)
            x1 = _matmul_residual(mixed, w_out[l], xb, mod, 2)
            h2 = _norm_modulate(x1, row(norm2_w[l]), mod, 3, 4)
            last = l == depth - 1
            xb = _conv_ffn(h2, w_ffn_up[l].astype(BF16), ffn_conv_w[l], ffn_conv_b[l], w_ffn_down[l].astype(BF16),
                           x1, mod, 5, norm_f_w, final_norm=last)
        outs.append(xb)
    return outs[0][None] if batch == 1 else jnp.stack(outs, axis=0)
```

```python
import functools
import math

import jax
import jax.numpy as jnp
from jax import lax
from jax.experimental import pallas as pl
from jax.experimental.pallas import tpu as pltpu

F32 = jnp.float32
BF16 = jnp.bfloat16
EPS = 1e-6
RG_C = 8.0
NEG_BIG = -1e30

V7X_LANES = 128
V7X_SUBLANES = 8
BF16_SUBLANES = 16
V7X_VMEM_LIMIT_BYTES = 60000 * 1024
MIB = 1024 * 1024


def _cparams(semantics, vmem_bytes):
    return pltpu.CompilerParams(
        dimension_semantics=semantics,
        vmem_limit_bytes=int(min(vmem_bytes, V7X_VMEM_LIMIT_BYTES)),
    )


def _tile(n, want, quantum):
    t = min(n, want) // quantum * quantum
    while t > quantum and n % t:
        t -= quantum
    assert t >= quantum and n % t == 0, (n, want, quantum)
    return t


def _rms_scale(x):
    return lax.rsqrt(jnp.mean(x * x, axis=-1, keepdims=True) + EPS)


def _adaln_mod_kernel(c_ref, w_ref, b_ref, o_ref):
    c = c_ref[...]
    a = c * jax.nn.sigmoid(c)
    a8 = jnp.broadcast_to(a, (V7X_SUBLANES, a.shape[1])).astype(BF16)
    acc = jnp.dot(a8, w_ref[...].astype(BF16), preferred_element_type=F32)
    o_ref[...] = acc[0:1, :] + b_ref[...]


def _adaln_mod(c, w, b):
    d, n = w.shape
    tn = _tile(n, 1024, V7X_LANES)
    return pl.pallas_call(
        _adaln_mod_kernel,
        grid=(n // tn,),
        in_specs=[
            pl.BlockSpec((1, d), lambda j: (0, 0)),
            pl.BlockSpec((d, tn), lambda j: (0, j)),
            pl.BlockSpec((1, tn), lambda j: (0, j)),
        ],
        out_specs=pl.BlockSpec((1, tn), lambda j: (0, j)),
        out_shape=jax.ShapeDtypeStruct((1, n), F32),
        compiler_params=_cparams(("arbitrary",), 2 * d * tn * 4 + d * tn * 2 + 4 * MIB),
        name="adaln_mod",
    )(c, w, b)


def _norm_modulate_kernel(x_ref, w_ref, scale_ref, shift_ref, o_ref):
    x = x_ref[...]
    y = x * _rms_scale(x) * w_ref[...]
    o_ref[...] = (y * (1.0 + scale_ref[...]) + shift_ref[...]).astype(o_ref.dtype)


def _norm_modulate(x, w, mod, shift_idx, scale_idx):
    s, d = x.shape
    tm = _tile(s, 256, BF16_SUBLANES)
    return pl.pallas_call(
        _norm_modulate_kernel,
        grid=(s // tm,),
        in_specs=[
            pl.BlockSpec((tm, d), lambda i: (i, 0)),
            pl.BlockSpec((1, d), lambda i: (0, 0)),
            pl.BlockSpec((1, d), lambda i: (0, scale_idx)),
            pl.BlockSpec((1, d), lambda i: (0, shift_idx)),
        ],
        out_specs=pl.BlockSpec((tm, d), lambda i: (i, 0)),
        out_shape=jax.ShapeDtypeStruct((s, d), BF16),
        compiler_params=_cparams(("parallel",), 2 * tm * d * 6 + 3 * tm * d * 4 + 2 * MIB),
        name="norm_modulate",
    )(x, w, mod, mod)


def _cast_weight_once(w_ref, wb_ref):
    @pl.when(pl.program_id(1) == 0)
    def _():
        wb_ref[...] = w_ref[...].astype(BF16)


def _matmul_kernel(x_ref, w_ref, o_ref, wb_ref):
    _cast_weight_once(w_ref, wb_ref)
    o_ref[...] = jnp.dot(x_ref[...], wb_ref[...], preferred_element_type=F32).astype(o_ref.dtype)


def _matmul_colscale_kernel(x_ref, w_ref, s_ref, o_ref, wb_ref):
    _cast_weight_once(w_ref, wb_ref)
    y = jnp.dot(x_ref[...], wb_ref[...], preferred_element_type=F32)
    o_ref[...] = (y * s_ref[...]).astype(o_ref.dtype)


def _matmul(x, w, col0, n, out_dtype, col_scale=None):
    m, k = x.shape
    tm = _tile(m, 1024, BF16_SUBLANES)
    tn = _tile(math.gcd(n, col0) if col0 else n, 512, V7X_LANES)
    ob = jnp.dtype(out_dtype).itemsize
    vmem = 2 * (tm * k * 2 + k * tn * 4 + tm * tn * ob) + k * tn * 2 + 2 * tm * tn * 4 + 2 * MIB
    in_specs = [
        pl.BlockSpec((tm, k), lambda j, i: (i, 0)),
        pl.BlockSpec((k, tn), lambda j, i: (0, col0 // tn + j)),
    ]
    args = (x, w)
    if col_scale is not None:
        in_specs.append(pl.BlockSpec((1, tn), lambda j, i: (0, j)))
        args = (x, w, col_scale)
    return pl.pallas_call(
        _matmul_kernel if col_scale is None else _matmul_colscale_kernel,
        grid=(n // tn, m // tm),
        in_specs=in_specs,
        out_specs=pl.BlockSpec((tm, tn), lambda j, i: (i, j)),
        out_shape=jax.ShapeDtypeStruct((m, n), out_dtype),
        scratch_shapes=[pltpu.VMEM((k, tn), BF16)],
        compiler_params=_cparams(("arbitrary", "arbitrary"), vmem),
        name="matmul",
    )(*args)


LOG2E = math.log2(math.e)
POS_RADIX = 32
N_BIAS_COLS = 6


def _split3_bf16(x):
    hi = x.astype(BF16).astype(F32)
    mid = (x - hi).astype(BF16).astype(F32)
    lo = (x - hi - mid).astype(BF16).astype(F32)
    return hi, mid, lo


def _diff_attn_kernel(q_ref, qn_ref, k_ref, v_ref, pos_ref, ones_ref, tri_ref, qc_ref, sh_ref, lq1_ref, lk1_ref, lq2_ref, lk2_ref,
                      sw_ref, o_ref, ka_ref, va_ref, m_ref, acc_ref, z_ref, zmax_ref, *, tq, tk, dh, lambda_init):
    i = pl.program_id(1)
    hw = 2 * dh
    s_len = k_ref.shape[0]

    @pl.when(i == 0)
    def _():
        lane_k = lax.broadcasted_iota(jnp.int32, (tk, hw), 1)

        def prep(j, carry):
            rows = pl.ds(pl.multiple_of(j * tk, tk), tk)
            kj = k_ref[rows, :]
            ka_ref[0, rows, :] = jnp.where(lane_k < dh, kj, pos_ref[0])
            ka_ref[1, rows, :] = jnp.where(lane_k >= dh, kj, pos_ref[1])
            va_ref[rows, 0:hw] = v_ref[rows, :]
            va_ref[rows, hw:2 * hw] = ones_ref[...]
            return carry

        lax.fori_loop(0, s_len // tk, prep, 0)

    lane_q = lax.broadcasted_iota(jnp.int32, (tq, hw), 1)
    qc = qc_ref[0].astype(BF16)

    def score_operands(q):
        return (jnp.where(lane_q < dh, q, qc[0:1, :]), jnp.where(lane_q >= dh, q, qc[1:2, :]))

    q_maps = score_operands(q_ref[...])
    frame_shift = sh_ref[0][:, 0:1]

    m_ref[...] = jnp.full(m_ref.shape, NEG_BIG, F32)
    acc_ref[...] = jnp.zeros(acc_ref.shape, F32)

    def scores(maps, j, c):
        rows = pl.ds(pl.multiple_of(j * tk, tk), tk)
        return lax.dot_general(maps[c], ka_ref[c, rows, :], (((1,), (1,)), ((), ())),
                               preferred_element_type=F32)

    def lanes(x, width):
        return jnp.concatenate([x] * (width // V7X_LANES), axis=-1)

    def row_max(z):
        return jnp.broadcast_to(jnp.max(z, axis=-1, keepdims=True), (tq, V7X_LANES))

    def update(j, c, z, z_max):
        width = z.shape[1]
        rows = pl.ds(pl.multiple_of(j * tk, tk), width)
        m_prev = m_ref[c] - frame_shift
        m_new = jnp.maximum(m_prev, z_max)
        alpha = jnp.exp2(m_prev - m_new)
        p = jnp.exp2(z - lanes(m_new, width)).astype(BF16)
        acc_ref[c] = lanes(alpha, 2 * hw) * acc_ref[c] + jnp.dot(p, va_ref[rows, :], preferred_element_type=F32)
        m_ref[c] = m_new

    n_full = (i * tq) // tk

    first = 2

    def issue_scores(maps, j, slot):
        for c in range(2):
            z = scores(maps, j, c)
            z_ref[slot, c] = z
            zmax_ref[slot, c] = row_max(z)

    @pl.when(i == 0)
    def _():
        issue_scores(q_maps, 0, first)

    def step(j, cur, nxt):
        issue_scores(q_maps, j + 1, nxt)
        for c in range(2):
            update(j, c, z_ref[cur, c], zmax_ref[cur, c])

    def next_tile_scores():
        issue_scores(score_operands(qn_ref[...]), 0, first)

    def diagonal_step(cur):
        for pos in range(tk // tq):
            @pl.when(i % (tk // tq) == pos)
            def _():
                if cur != first:
                    next_tile_scores()
                lo, width = pos * tq, (pos + 1) * tq
                for c in range(2):
                    diag = z_ref[cur, c, :, lo:width] + tri_ref[...]
                    z = diag if pos == 0 else jnp.concatenate([z_ref[cur, c, :, 0:lo], diag], axis=-1)
                    update(n_full, c, z, row_max(z))
                if cur == first:
                    next_tile_scores()

    @pl.when(n_full >= 1)
    def _():
        step(0, first, 0)

    def chunk_pair(t, carry):
        step(1 + 2 * t, 0, 1)
        step(2 + 2 * t, 1, 0)
        return carry

    lax.fori_loop(0, (n_full - 1) // 2, chunk_pair, 0)

    @pl.when(n_full == 0)
    def _():
        diagonal_step(first)

    @pl.when(n_full % 2 == 1)
    def _():
        diagonal_step(0)

    @pl.when(jnp.logical_and(n_full >= 2, n_full % 2 == 0))
    def _():
        step(n_full - 1, 0, 1)
        diagonal_step(1)

    a0 = acc_ref[0]
    a1 = acc_ref[1]
    o1 = a0[:, 0:hw] / a0[:, hw:2 * hw]
    o2 = a1[:, 0:hw] / a1[:, hw:2 * hw]
    lam = (jnp.exp(jnp.sum(lq1_ref[...] * lk1_ref[...], axis=-1, keepdims=True))
           - jnp.exp(jnp.sum(lq2_ref[...] * lk2_ref[...], axis=-1, keepdims=True)) + lambda_init)
    o = o1 - lam * o2
    y = o * _rms_scale(o) * sw_ref[...]
    o_ref[...] = (y * (1.0 - lambda_init)).astype(o_ref.dtype)


def _diff_attention(qkv, lq1, lk1, lq2, lk2, subln_w, lambda_init, n_heads, dh):
    s = qkv.shape[0]
    hw = 2 * dh
    tk = _tile(s, 1024, V7X_LANES)
    tq = _tile(tk, 512, BF16_SUBLANES)
    nb = N_BIAS_COLS
    assert dh >= nb and tk // POS_RADIX <= 256
    slopes2 = (2.0 ** (-8.0 * jnp.arange(1, n_heads + 1, dtype=F32) / n_heads)) * LOG2E
    pieces = jnp.stack(_split3_bf16(slopes2), axis=-1)
    q_cols = jnp.concatenate([pieces * POS_RADIX, pieces], axis=-1)
    qc = jnp.zeros((n_heads, 2, hw), F32).at[:, 0, dh:dh + nb].set(q_cols).at[:, 1, 0:nb].set(q_cols)
    jj = jnp.arange(tk, dtype=jnp.int32)
    k_cols = jnp.stack([jj // POS_RADIX] * 3 + [jj % POS_RADIX] * 3, axis=-1).astype(F32)
    pos = jnp.zeros((2, tk, hw), F32).at[0, :, dh:dh + nb].set(k_cols).at[1, :, 0:nb].set(k_cols).astype(BF16)
    ones = jnp.ones((tk, hw), BF16)
    shift = jnp.broadcast_to((slopes2 * tk)[:, None, None], (n_heads, 1, V7X_LANES))
    rel = jnp.arange(tq, dtype=jnp.int32)[None, :] - jnp.arange(tq, dtype=jnp.int32)[:, None]
    tri_bias = jnp.where(rel <= 0, 0.0, NEG_BIG).astype(F32)
    kern = functools.partial(_diff_attn_kernel, tq=tq, tk=tk, dh=dh, lambda_init=lambda_init)
    vec = lambda a: a.reshape(1, -1).astype(F32)
    const2 = lambda h, i: (0, 0)
    vmem = (2 * (2 * s * hw * 2 + 2 * tq * hw * 2 + 3 * tk * hw * 2) + 4 * s * hw * 2
            + 8 * tq * tk * 4 + 2 * tq * (2 * hw + V7X_LANES) * 4 + 4 * MIB)
    return pl.pallas_call(
        kern,
        grid=(n_heads, s // tq),
        in_specs=[
            pl.BlockSpec((tq, hw), lambda h, i: (i, h)),
            pl.BlockSpec((tq, hw), lambda h, i: (jnp.minimum(i + 1, s // tq - 1), h)),
            pl.BlockSpec((s, hw), lambda h, i: (0, n_heads + h)),
            pl.BlockSpec((s, hw), lambda h, i: (0, 2 * n_heads + h)),
            pl.BlockSpec((2, tk, hw), lambda h, i: (0, 0, 0)),
            pl.BlockSpec((tk, hw), const2),
            pl.BlockSpec((tq, tq), const2),
            pl.BlockSpec((1, 2, hw), lambda h, i: (h, 0, 0)),
            pl.BlockSpec((1, 1, V7X_LANES), lambda h, i: (h, 0, 0)),
            pl.BlockSpec((1, dh), const2),
            pl.BlockSpec((1, dh), const2),
            pl.BlockSpec((1, dh), const2),
            pl.BlockSpec((1, dh), const2),
            pl.BlockSpec((1, hw), const2),
        ],
        out_specs=pl.BlockSpec((tq, hw), lambda h, i: (i, h)),
        out_shape=jax.ShapeDtypeStruct((s, n_heads * hw), BF16),
        scratch_shapes=[
            pltpu.VMEM((2, s, hw), BF16),
            pltpu.VMEM((s, 2 * hw), BF16),
            pltpu.VMEM((2, tq, V7X_LANES), F32),
            pltpu.VMEM((2, tq, 2 * hw), F32),
            pltpu.VMEM((3, 2, tq, tk), F32),
            pltpu.VMEM((3, 2, tq, V7X_LANES), F32),
        ],
        compiler_params=_cparams(("arbitrary", "arbitrary"), vmem),
        name="diff_attention",
    )(qkv, qkv, qkv, qkv, pos, ones, tri_bias, qc, shift, vec(lq1), vec(lk1), vec(lq2), vec(lk2), vec(subln_w))


def _rg_lru_kernel(rx_ref, rg_ref, cw_ref, cb_ref, wa_ref, ba_ref, wx_ref, bx_ref, lam_ref, o_ref,
                   ext_ref, a_ref, b_ref, tail_ref, h_ref, *, t_rows, cw, blk):
    t = pl.program_id(1)
    halo = V7X_SUBLANES

    @pl.when(t == 0)
    def _():
        tail_ref[...] = jnp.zeros(tail_ref.shape, F32)
        h_ref[...] = jnp.zeros(h_ref.shape, F32)

    rx = rx_ref[...]
    ext_ref[0:halo, :] = tail_ref[...]
    ext_ref[halo:, :] = rx
    tail_ref[...] = rx[t_rows - halo:, :]
    ext = ext_ref[...]
    taps = cw_ref[...]
    n_taps = taps.shape[0]
    xr = cb_ref[...]
    for j in range(n_taps):
        lag = n_taps - 1 - j
        xs = rx if lag == 0 else pltpu.roll(ext, lag, 0)[halo:, :]
        xr = xr + xs * taps[j:j + 1, :]

    ra, ia = [], []
    for g in range(cw // blk):
        xb = xr[:, g * blk:(g + 1) * blk].astype(BF16)
        ra.append(jnp.dot(xb, wa_ref[g], preferred_element_type=F32))
        ia.append(jnp.dot(xb, wx_ref[g], preferred_element_type=F32))
    r = jax.nn.sigmoid(jnp.concatenate(ra, axis=-1) + ba_ref[...])
    ig = jax.nn.sigmoid(jnp.concatenate(ia, axis=-1) + bx_ref[...])
    neg_lam = -lam_ref[...]
    softplus = jnp.maximum(neg_lam, 0.0) + jnp.log1p(jnp.exp(-jnp.abs(neg_lam)))
    log_a = (-RG_C) * r * softplus
    th = jnp.tanh(log_a)
    a_ref[...] = jnp.exp(log_a)
    u = -2.0 * th / (1.0 - th)
    b_ref[...] = jnp.where(u > 0.0, u * lax.rsqrt(u), 0.0) * (ig * xr)

    row = lax.broadcasted_iota(jnp.int32, (V7X_SUBLANES, cw), 0)

    def step(n, h):
        rows = pl.ds(pl.multiple_of(n * V7X_SUBLANES, V7X_SUBLANES), V7X_SUBLANES)
        a = a_ref[rows, :]
        b = b_ref[rows, :]
        d = 1
        while d < V7X_SUBLANES:
            keep = row >= d
            a_s = jnp.where(keep, pltpu.roll(a, d, 0), 1.0)
            b_s = jnp.where(keep, pltpu.roll(b, d, 0), 0.0)
            b = a * b_s + b
            a = a * a_s
            d *= 2
        hh = a * h + b
        b_ref[rows, :] = hh
        return hh[V7X_SUBLANES - 1:, :]

    h_last = lax.fori_loop(0, t_rows // V7X_SUBLANES, step, h_ref[0:1, :], unroll=4)
    h_ref[...] = jnp.broadcast_to(h_last, h_ref.shape)
    o_ref[...] = (b_ref[...] * jax.nn.gelu(rg_ref[...])).astype(o_ref.dtype)


def _rg_lru(rest, conv_w, conv_b, w_a, b_a, w_x, b_x, lru_lambda, d_rnn):
    s = rest.shape[0]
    n_blk, blk, _ = w_a.shape
    cw = _tile(d_rnn, 512, blk)
    t_rows = _tile(s, 1024, BF16_SUBLANES)
    n_cb = d_rnn // cw
    row = lambda a: a.reshape(1, -1).astype(F32)
    kern = functools.partial(_rg_lru_kernel, t_rows=t_rows, cw=cw, blk=blk)
    vec_spec = pl.BlockSpec((1, cw), lambda n, t: (0, n))
    gate_spec = pl.BlockSpec((cw // blk, blk, blk), lambda n, t: (n, 0, 0))
    vmem = 2 * (2 * t_rows * cw * 4 + t_rows * cw * 2) + 12 * t_rows * cw * 4 + 4 * MIB
    return pl.pallas_call(
        kern,
        grid=(n_cb, s // t_rows),
        in_specs=[
            pl.BlockSpec((t_rows, cw), lambda n, t: (t, n)),
            pl.BlockSpec((t_rows, cw), lambda n, t: (t, n_cb + n)),
            pl.BlockSpec((conv_w.shape[0], cw), lambda n, t: (0, n)),
            vec_spec, gate_spec, vec_spec, gate_spec, vec_spec, vec_spec,
        ],
        out_specs=pl.BlockSpec((t_rows, cw), lambda n, t: (t, n)),
        out_shape=jax.ShapeDtypeStruct((s, d_rnn), BF16),
        scratch_shapes=[
            pltpu.VMEM((t_rows + V7X_SUBLANES, cw), F32),
            pltpu.VMEM((t_rows, cw), F32),
            pltpu.VMEM((t_rows, cw), F32),
            pltpu.VMEM((V7X_SUBLANES, cw), F32),
            pltpu.VMEM((V7X_SUBLANES, cw), F32),
        ],
        compiler_params=_cparams(("parallel", "arbitrary"), vmem),
        name="rg_lru",
    )(rest, rest, conv_w.astype(F32), row(conv_b), w_a.astype(BF16), row(b_a), w_x.astype(BF16), row(b_x),
      row(lru_lambda))


def _gated_merge_kernel(ya_ref, yr_ref, wa_ref, wr_ref, ga_ref, gb_ref, o_ref, wab_ref, wrb_ref):
    _cast_weight_once(wa_ref, wab_ref)
    _cast_weight_once(wr_ref, wrb_ref)
    pa = jnp.dot(ya_ref[...], wab_ref[...], preferred_element_type=F32)
    pr = jnp.dot(yr_ref[...], wrb_ref[...], preferred_element_type=F32)
    o_ref[...] = (jax.nn.sigmoid(ga_ref[...]) * pa + jax.nn.sigmoid(gb_ref[...]) * pr).astype(o_ref.dtype)


def _gated_merge(y_att, y_rec, w_pa, w_pr, rest, ga_col, gb_col):
    m, ka = y_att.shape
    _, kr = y_rec.shape
    n = w_pa.shape[1]
    tm = _tile(m, 1024, BF16_SUBLANES)
    tn = _tile(math.gcd(math.gcd(n, ga_col), gb_col), 512, V7X_LANES)
    vmem = (2 * (tm * (ka + kr) * 2 + (ka + kr) * tn * 4 + 2 * tm * tn * 4 + tm * tn * 2) + (ka + kr) * tn * 2
            + 4 * tm * tn * 4 + 2 * MIB)
    return pl.pallas_call(
        _gated_merge_kernel,
        grid=(n // tn, m // tm),
        in_specs=[
            pl.BlockSpec((tm, ka), lambda j, i: (i, 0)),
            pl.BlockSpec((tm, kr), lambda j, i: (i, 0)),
            pl.BlockSpec((ka, tn), lambda j, i: (0, j)),
            pl.BlockSpec((kr, tn), lambda j, i: (0, j)),
            pl.BlockSpec((tm, tn), lambda j, i: (i, ga_col // tn + j)),
            pl.BlockSpec((tm, tn), lambda j, i: (i, gb_col // tn + j)),
        ],
        out_specs=pl.BlockSpec((tm, tn), lambda j, i: (i, j)),
        out_shape=jax.ShapeDtypeStruct((m, n), BF16),
        scratch_shapes=[pltpu.VMEM((ka, tn), BF16), pltpu.VMEM((kr, tn), BF16)],
        compiler_params=_cparams(("arbitrary", "arbitrary"), vmem),
        name="gated_merge",
    )(y_att, y_rec, w_pa, w_pr, rest, rest)


def _matmul_residual_kernel(a_ref, w_ref, x_ref, g_ref, o_ref, wb_ref):
    _cast_weight_once(w_ref, wb_ref)
    y = jnp.dot(a_ref[...], wb_ref[...], preferred_element_type=F32)
    o_ref[...] = x_ref[...] + g_ref[...] * y


def _matmul_residual(a, w, x, mod, gate_idx):
    m, k = a.shape
    n = w.shape[1]
    tm = _tile(m, 1024, BF16_SUBLANES)
    tn = _tile(n, 512, V7X_LANES)
    vmem = 2 * (tm * k * 2 + k * tn * 4 + 2 * tm * tn * 4) + k * tn * 2 + tm * tn * 4 + 2 * MIB
    return pl.pallas_call(
        _matmul_residual_kernel,
        grid=(n // tn, m // tm),
        in_specs=[
            pl.BlockSpec((tm, k), lambda j, i: (i, 0)),
            pl.BlockSpec((k, tn), lambda j, i: (0, j)),
            pl.BlockSpec((tm, tn), lambda j, i: (i, j)),
            pl.BlockSpec((1, tn), lambda j, i: (0, gate_idx * (n // tn) + j)),
        ],
        out_specs=pl.BlockSpec((tm, tn), lambda j, i: (i, j)),
        out_shape=jax.ShapeDtypeStruct((m, n), F32),
        scratch_shapes=[pltpu.VMEM((k, tn), BF16)],
        compiler_params=_cparams(("arbitrary", "arbitrary"), vmem),
        name="matmul_residual",
    )(a, w, x, mod)


def _conv_ffn_kernel(h_ref, wg_ref, wu_ref, cw_ref, cb_ref, wd_ref, x_ref, g_ref, nw_ref, o_ref,
                     gate_ref, up_ref, act_ref, carry_ref, *, tm, dn, rn, rc, nf, final_norm):
    mi = pl.program_id(0)
    f = pl.program_id(1)
    halo = V7X_SUBLANES
    d = o_ref.shape[1]

    def gate_up(slot):
        @pl.when(mi == 0)
        def _():
            carry_ref[f] = jnp.zeros(carry_ref.shape[1:], F32)

        h = h_ref[...]
        gate = jnp.dot(h, wg_ref[...], preferred_element_type=F32)
        up_ref[slot] = jnp.dot(h, wu_ref[...], preferred_element_type=F32)
        gate_ref[slot, 0:halo, :] = carry_ref[f]
        gate_ref[slot, halo:, :] = gate
        carry_ref[f] = gate[tm - halo:, :]

    def activation_rows(slot, r):
        r0 = r * rc
        taps = cw_ref[f - 1]
        n_taps = taps.shape[0]
        gc = cb_ref[f - 1]
        for j in range(n_taps):
            start = halo - (n_taps - 1 - j) + r0
            gc = gc + gate_ref[slot, start:start + rc, :] * taps[j:j + 1, :]
        act_ref[slot, r0:r0 + rc, :] = (jax.nn.gelu(gc) * up_ref[slot, r0:r0 + rc, :]).astype(BF16)

    def down_cols(slot, c):
        cols = slice(c * dn, (c + 1) * dn)
        o_ref[:, cols] += jnp.dot(act_ref[slot], wd_ref[:, cols], preferred_element_type=F32)

    def step(gu_slot, act_slot, down_slot):
        n_c = d // dn
        n_r = tm // rc
        if gu_slot is not None:
            gate_up(gu_slot)
        for k in range(max(n_c, n_r)):
            if act_slot is not None and k < n_r:
                activation_rows(act_slot, k)
            if down_slot is not None and k < n_c:
                down_cols(down_slot, k)

    @pl.when(f == 0)
    def _():
        o_ref[...] = jnp.zeros(o_ref.shape, F32)
        step(0, None, None)

    @pl.when(f == 1)
    def _():
        step(1, 0, None)

    @pl.when(jnp.logical_and(jnp.logical_and(f >= 2, f < nf), f % 2 == 0))
    def _():
        step(0, 1, 0)

    @pl.when(jnp.logical_and(jnp.logical_and(f >= 2, f < nf), f % 2 == 1))
    def _():
        step(1, 0, 1)

    @pl.when(f == nf)
    def _():
        step(None, (nf - 1) % 2, nf % 2)

    @pl.when(f == nf + 1)
    def _():
        step(None, None, (nf - 1) % 2)
        for r in range(tm // rn):
            rows = slice(r * rn, (r + 1) * rn)
            x2 = x_ref[rows, :] + g_ref[...] * o_ref[rows, :]
            if final_norm:
                x2 = x2 * _rms_scale(x2) * nw_ref[...]
            o_ref[rows, :] = x2


def _conv_ffn(h, w_up, conv_w, conv_b, w_down, x, mod, gate_idx, norm_w, final_norm):
    s, d = h.shape
    d_ff = w_down.shape[0]
    tm = _tile(s, 512, BF16_SUBLANES)
    tf = _tile(d_ff, 256, V7X_LANES)
    nf = d_ff // tf
    assert nf >= 2
    dn = _tile(d, 512, V7X_LANES)
    rn = _tile(tm, 128, V7X_SUBLANES)
    rc = _tile(tm, 64, BF16_SUBLANES)
    kern = functools.partial(_conv_ffn_kernel, tm=tm, dn=dn, rn=rn, rc=rc, nf=nf, final_norm=final_norm)
    vmem = (2 * (tm * d * 2 + 2 * d * tf * 2 + tf * d * 2 + tm * d * 4) + tm * d * 4
            + 12 * tm * tf * 4 + 4 * rn * d * 4 + tm * dn * 4 + 2 * MIB)
    chunk = lambda f, lag: jnp.clip(f - lag, 0, nf - 1)
    n_taps = conv_w.shape[0]
    taps_by_chunk = conv_w.astype(F32).reshape(n_taps, nf, tf).transpose(1, 0, 2)
    bias_by_chunk = conv_b.astype(F32).reshape(nf, 1, tf)
    return pl.pallas_call(
        kern,
        grid=(s // tm, nf + 2),
        in_specs=[
            pl.BlockSpec((tm, d), lambda i, f: (i, 0)),
            pl.BlockSpec((d, tf), lambda i, f: (0, chunk(f, 0))),
            pl.BlockSpec((d, tf), lambda i, f: (0, nf + chunk(f, 0))),
            pl.BlockSpec((nf, n_taps, tf), lambda i, f: (0, 0, 0)),
            pl.BlockSpec((nf, 1, tf), lambda i, f: (0, 0, 0)),
            pl.BlockSpec((tf, d), lambda i, f: (chunk(f, 2), 0)),
            pl.BlockSpec((tm, d), lambda i, f: (i, 0), pipeline_mode=pl.Buffered(1)),
            pl.BlockSpec((1, d), lambda i, f: (0, gate_idx)),
            pl.BlockSpec((1, d), lambda i, f: (0, 0)),
        ],
        out_specs=pl.BlockSpec((tm, d), lambda i, f: (i, 0)),
        out_shape=jax.ShapeDtypeStruct((s, d), F32),
        scratch_shapes=[
            pltpu.VMEM((2, tm + V7X_SUBLANES, tf), F32),
            pltpu.VMEM((2, tm, tf), F32),
            pltpu.VMEM((2, tm, tf), BF16),
            pltpu.VMEM((nf, V7X_SUBLANES, tf), F32),
        ],
        compiler_params=_cparams(("arbitrary", "arbitrary"), vmem),
        name="conv_ffn",
    )(h, w_up, w_up, taps_by_chunk, bias_by_chunk, w_down, x, mod, norm_w.reshape(1, -1).astype(F32))


def kernel(x, c, w_ada, b_ada, norm1_w, w_in, lambda_q1, lambda_k1, lambda_q2, lambda_k2, subln_w, conv_w, conv_b, w_rg_a, b_rg_a, w_rg_x, b_rg_x, lru_lambda, w_proj_attn, w_proj_rec, w_out, norm2_w, w_ffn_up, ffn_conv_w, ffn_conv_b, w_ffn_down, norm_f_w):
    batch, seq, d = x.shape
    depth = w_ada.shape[0]
    dh = lambda_q1.shape[-1]
    attn_w = w_proj_attn.shape[1]
    n_heads = attn_w // (2 * dh)
    d_rnn = conv_w.shape[-1]
    assert w_in.shape[-1] == 3 * attn_w + 2 * d_rnn + 2 * d
    row = lambda a: a.reshape(1, -1).astype(F32)

    outs = []
    for b in range(batch):
        xb = x[b]
        cb = c[b:b + 1]
        for l in range(depth):
            lambda_init = 0.8 - 0.6 * math.exp(-0.3 * l)
            mod = _adaln_mod(cb, w_ada[l], row(b_ada[l]))
            h1 = _norm_modulate(xb, row(norm1_w[l]), mod, 0, 1)
            q_scale = jnp.concatenate([jnp.full((1, attn_w), dh ** -0.5 * LOG2E, F32),
                                       jnp.ones((1, 2 * attn_w), F32)], axis=1)
            qkv = _matmul(h1, w_in[l], 0, 3 * attn_w, BF16, q_scale)
            rest = _matmul(h1, w_in[l], 3 * attn_w, 2 * d_rnn + 2 * d, F32)
            y_att = _diff_attention(qkv, lambda_q1[l], lambda_k1[l], lambda_q2[l], lambda_k2[l], subln_w[l],
                                    lambda_init, n_heads, dh)
            y_rec = _rg_lru(rest, conv_w[l], conv_b[l], w_rg_a[l], b_rg_a[l], w_rg_x[l], b_rg_x[l],
                            lru_lambda[l], d_rnn)
            mixed = _gated_merge(y_att, y_rec, w_proj_attn[l], w_proj_rec[l], rest, 2 * d_rnn, 2 * d_rnn + d)
            x1 = _matmul_residual(mixed, w_out[l], xb, mod, 2)
            h2 = _norm_modulate(x1, row(norm2_w[l]), mod, 3, 4)
            last = l == depth - 1
            xb = _conv_ffn(h2, w_ffn_up[l].astype(BF16), ffn_conv_w[l], ffn_conv_b[l], w_ffn_down[l].astype(BF16),
                           x1, mod, 5, norm_f_w, final_norm=last)
        outs.append(xb)
    return outs[0][None] if batch == 1 else jnp.stack(outs, axis=0)
```

```python
import functools
import math

import jax
import jax.numpy as jnp
from jax import lax
from jax.experimental import pallas as pl
from jax.experimental.pallas import tpu as pltpu

F32 = jnp.float32
BF16 = jnp.bfloat16
EPS = 1e-6
RG_C = 8.0
NEG_BIG = -1e30

V7X_LANES = 128
V7X_SUBLANES = 8
BF16_SUBLANES = 16
V7X_VMEM_LIMIT_BYTES = 60000 * 1024
MIB = 1024 * 1024


def _cparams(semantics, vmem_bytes):
    return pltpu.CompilerParams(
        dimension_semantics=semantics,
        vmem_limit_bytes=int(min(vmem_bytes, V7X_VMEM_LIMIT_BYTES)),
    )


def _tile(n, want, quantum):
    t = min(n, want) // quantum * quantum
    while t > quantum and n % t:
        t -= quantum
    assert t >= quantum and n % t == 0, (n, want, quantum)
    return t


def _rms_scale(x):
    return lax.rsqrt(jnp.mean(x * x, axis=-1, keepdims=True) + EPS)


def _adaln_mod_kernel(c_ref, w_ref, b_ref, o_ref):
    c = c_ref[...]
    a = c * jax.nn.sigmoid(c)
    a8 = jnp.broadcast_to(a, (V7X_SUBLANES, a.shape[1])).astype(BF16)
    acc = jnp.dot(a8, w_ref[...].astype(BF16), preferred_element_type=F32)
    o_ref[...] = acc[0:1, :] + b_ref[...]


def _adaln_mod(c, w, b):
    d, n = w.shape
    tn = _tile(n, 1024, V7X_LANES)
    return pl.pallas_call(
        _adaln_mod_kernel,
        grid=(n // tn,),
        in_specs=[
            pl.BlockSpec((1, d), lambda j: (0, 0)),
            pl.BlockSpec((d, tn), lambda j: (0, j)),
            pl.BlockSpec((1, tn), lambda j: (0, j)),
        ],
        out_specs=pl.BlockSpec((1, tn), lambda j: (0, j)),
        out_shape=jax.ShapeDtypeStruct((1, n), F32),
        compiler_params=_cparams(("arbitrary",), 2 * d * tn * 4 + d * tn * 2 + 4 * MIB),
        name="adaln_mod",
    )(c, w, b)


def _norm_modulate_kernel(x_ref, w_ref, scale_ref, shift_ref, o_ref):
    x = x_ref[...]
    y = x * _rms_scale(x) * w_ref[...]
    o_ref[...] = (y * (1.0 + scale_ref[...]) + shift_ref[...]).astype(o_ref.dtype)


def _norm_modulate(x, w, mod, shift_idx, scale_idx):
    s, d = x.shape
    tm = _tile(s, 256, BF16_SUBLANES)
    return pl.pallas_call(
        _norm_modulate_kernel,
        grid=(s // tm,),
        in_specs=[
            pl.BlockSpec((tm, d), lambda i: (i, 0)),
            pl.BlockSpec((1, d), lambda i: (0, 0)),
            pl.BlockSpec((1, d), lambda i: (0, scale_idx)),
            pl.BlockSpec((1, d), lambda i: (0, shift_idx)),
        ],
        out_specs=pl.BlockSpec((tm, d), lambda i: (i, 0)),
        out_shape=jax.ShapeDtypeStruct((s, d), BF16),
        compiler_params=_cparams(("parallel",), 2 * tm * d * 6 + 3 * tm * d * 4 + 2 * MIB),
        name="norm_modulate",
    )(x, w, mod, mod)


def _cast_weight_once(w_ref, wb_ref):
    @pl.when(pl.program_id(1) == 0)
    def _():
        wb_ref[...] = w_ref[...].astype(BF16)


def _matmul_kernel(x_ref, w_ref, o_ref, wb_ref):
    _cast_weight_once(w_ref, wb_ref)
    o_ref[...] = jnp.dot(x_ref[...], wb_ref[...], preferred_element_type=F32).astype(o_ref.dtype)


def _matmul_colscale_kernel(x_ref, w_ref, s_ref, o_ref, wb_ref):
    _cast_weight_once(w_ref, wb_ref)
    y = jnp.dot(x_ref[...], wb_ref[...], preferred_element_type=F32)
    o_ref[...] = (y * s_ref[...]).astype(o_ref.dtype)


def _matmul(x, w, col0, n, out_dtype, col_scale=None):
    m, k = x.shape
    tm = _tile(m, 1024, BF16_SUBLANES)
    tn = _tile(math.gcd(n, col0) if col0 else n, 512, V7X_LANES)
    ob = jnp.dtype(out_dtype).itemsize
    vmem = 2 * (tm * k * 2 + k * tn * 4 + tm * tn * ob) + k * tn * 2 + 2 * tm * tn * 4 + 2 * MIB
    in_specs = [
        pl.BlockSpec((tm, k), lambda j, i: (i, 0)),
        pl.BlockSpec((k, tn), lambda j, i: (0, col0 // tn + j)),
    ]
    args = (x, w)
    if col_scale is not None:
        in_specs.append(pl.BlockSpec((1, tn), lambda j, i: (0, j)))
        args = (x, w, col_scale)
    return pl.pallas_call(
        _matmul_kernel if col_scale is None else _matmul_colscale_kernel,
        grid=(n // tn, m // tm),
        in_specs=in_specs,
        out_specs=pl.BlockSpec((tm, tn), lambda j, i: (i, j)),
        out_shape=jax.ShapeDtypeStruct((m, n), out_dtype),
        scratch_shapes=[pltpu.VMEM((k, tn), BF16)],
        compiler_params=_cparams(("arbitrary", "arbitrary"), vmem),
        name="matmul",
    )(*args)


LOG2E = math.log2(math.e)
POS_RADIX = 32
N_BIAS_COLS = 6


def _split3_bf16(x):
    hi = x.astype(BF16).astype(F32)
    mid = (x - hi).astype(BF16).astype(F32)
    lo = (x - hi - mid).astype(BF16).astype(F32)
    return hi, mid, lo


def _diff_attn_kernel(q_ref, k_ref, v_ref, pos_ref, ones_ref, tri_ref, qc_ref, sh_ref, lq1_ref, lk1_ref, lq2_ref, lk2_ref,
                      sw_ref, o_ref, ka_ref, va_ref, m_ref, acc_ref, z_ref, zmax_ref, *, tq, tk, dh, lambda_init):
    i = pl.program_id(1)
    hw = 2 * dh
    s_len = k_ref.shape[0]

    @pl.when(i == 0)
    def _():
        lane_k = lax.broadcasted_iota(jnp.int32, (tk, hw), 1)

        def prep(j, carry):
            rows = pl.ds(pl.multiple_of(j * tk, tk), tk)
            kj = k_ref[rows, :]
            ka_ref[0, rows, :] = jnp.where(lane_k < dh, kj, pos_ref[0])
            ka_ref[1, rows, :] = jnp.where(lane_k >= dh, kj, pos_ref[1])
            va_ref[rows, 0:hw] = v_ref[rows, :]
            va_ref[rows, hw:2 * hw] = ones_ref[...]
            return carry

        lax.fori_loop(0, s_len // tk, prep, 0)

    lane_q = lax.broadcasted_iota(jnp.int32, (tq, hw), 1)
    qc = qc_ref[0].astype(BF16)

    def score_operands(q):
        return (jnp.where(lane_q < dh, q, qc[0:1, :]), jnp.where(lane_q >= dh, q, qc[1:2, :]))

    q_maps = score_operands(q_ref[...])
    frame_shift = sh_ref[0][:, 0:1]

    m_ref[...] = jnp.full(m_ref.shape, NEG_BIG, F32)
    acc_ref[...] = jnp.zeros(acc_ref.shape, F32)

    def scores(maps, j, c):
        rows = pl.ds(pl.multiple_of(j * tk, tk), tk)
        return lax.dot_general(maps[c], ka_ref[c, rows, :], (((1,), (1,)), ((), ())),
                               preferred_element_type=F32)

    def lanes(x, width):
        return jnp.concatenate([x] * (width // V7X_LANES), axis=-1)

    def row_max(z):
        return jnp.broadcast_to(jnp.max(z, axis=-1, keepdims=True), (tq, V7X_LANES))

    def update(j, c, z, z_max):
        width = z.shape[1]
        rows = pl.ds(pl.multiple_of(j * tk, tk), width)
        m_prev = m_ref[c] - frame_shift
        m_new = jnp.maximum(m_prev, z_max)
        alpha = jnp.exp2(m_prev - m_new)
        p = jnp.exp2(z - lanes(m_new, width)).astype(BF16)
        acc_ref[c] = lanes(alpha, 2 * hw) * acc_ref[c] + jnp.dot(p, va_ref[rows, :], preferred_element_type=F32)
        m_ref[c] = m_new

    n_full = (i * tq) // tk

    def issue_scores(j, slot):
        for c in range(2):
            z = scores(q_maps, j, c)
            z_ref[slot, c] = z
            zmax_ref[slot, c] = row_max(z)

    def step(j, cur):
        issue_scores(j + 1, 1 - cur)
        for c in range(2):
            update(j, c, z_ref[cur, c], zmax_ref[cur, c])

    def diagonal_step(cur):
        for pos in range(tk // tq):
            @pl.when(i % (tk // tq) == pos)
            def _():
                lo, width = pos * tq, (pos + 1) * tq
                for c in range(2):
                    diag = z_ref[cur, c, :, lo:width] + tri_ref[...]
                    z = diag if pos == 0 else jnp.concatenate([z_ref[cur, c, :, 0:lo], diag], axis=-1)
                    update(n_full, c, z, row_max(z))

    issue_scores(0, 0)

    def chunk_pair(t, carry):
        step(2 * t, 0)
        step(2 * t + 1, 1)
        return carry

    lax.fori_loop(0, n_full // 2, chunk_pair, 0)

    @pl.when(n_full % 2 == 1)
    def _():
        step(n_full - 1, 0)
        diagonal_step(1)

    @pl.when(n_full % 2 == 0)
    def _():
        diagonal_step(0)

    a0 = acc_ref[0]
    a1 = acc_ref[1]
    o1 = a0[:, 0:hw] / a0[:, hw:2 * hw]
    o2 = a1[:, 0:hw] / a1[:, hw:2 * hw]
    lam = (jnp.exp(jnp.sum(lq1_ref[...] * lk1_ref[...], axis=-1, keepdims=True))
           - jnp.exp(jnp.sum(lq2_ref[...] * lk2_ref[...], axis=-1, keepdims=True)) + lambda_init)
    o = o1 - lam * o2
    y = o * _rms_scale(o) * sw_ref[...]
    o_ref[...] = (y * (1.0 - lambda_init)).astype(o_ref.dtype)


def _diff_attention(qkv, lq1, lk1, lq2, lk2, subln_w, lambda_init, n_heads, dh):
    s = qkv.shape[0]
    hw = 2 * dh
    tk = _tile(s, 1024, V7X_LANES)
    tq = _tile(tk, 512, BF16_SUBLANES)
    nb = N_BIAS_COLS
    assert dh >= nb and tk // POS_RADIX <= 256
    slopes2 = (2.0 ** (-8.0 * jnp.arange(1, n_heads + 1, dtype=F32) / n_heads)) * LOG2E
    pieces = jnp.stack(_split3_bf16(slopes2), axis=-1)
    q_cols = jnp.concatenate([pieces * POS_RADIX, pieces], axis=-1)
    qc = jnp.zeros((n_heads, 2, hw), F32).at[:, 0, dh:dh + nb].set(q_cols).at[:, 1, 0:nb].set(q_cols)
    jj = jnp.arange(tk, dtype=jnp.int32)
    k_cols = jnp.stack([jj // POS_RADIX] * 3 + [jj % POS_RADIX] * 3, axis=-1).astype(F32)
    pos = jnp.zeros((2, tk, hw), F32).at[0, :, dh:dh + nb].set(k_cols).at[1, :, 0:nb].set(k_cols).astype(BF16)
    ones = jnp.ones((tk, hw), BF16)
    shift = jnp.broadcast_to((slopes2 * tk)[:, None, None], (n_heads, 1, V7X_LANES))
    rel = jnp.arange(tq, dtype=jnp.int32)[None, :] - jnp.arange(tq, dtype=jnp.int32)[:, None]
    tri_bias = jnp.where(rel <= 0, 0.0, NEG_BIG).astype(F32)
    kern = functools.partial(_diff_attn_kernel, tq=tq, tk=tk, dh=dh, lambda_init=lambda_init)
    vec = lambda a: a.reshape(1, -1).astype(F32)
    const2 = lambda h, i: (0, 0)
    vmem = (2 * (2 * s * hw * 2 + 2 * tq * hw * 2 + 3 * tk * hw * 2) + 4 * s * hw * 2
            + 8 * tq * tk * 4 + 2 * tq * (2 * hw + V7X_LANES) * 4 + 4 * MIB)
    return pl.pallas_call(
        kern,
        grid=(n_heads, s // tq),
        in_specs=[
            pl.BlockSpec((tq, hw), lambda h, i: (i, h)),
            pl.BlockSpec((s, hw), lambda h, i: (0, n_heads + h)),
            pl.BlockSpec((s, hw), lambda h, i: (0, 2 * n_heads + h)),
            pl.BlockSpec((2, tk, hw), lambda h, i: (0, 0, 0)),
            pl.BlockSpec((tk, hw), const2),
            pl.BlockSpec((tq, tq), const2),
            pl.BlockSpec((1, 2, hw), lambda h, i: (h, 0, 0)),
            pl.BlockSpec((1, 1, V7X_LANES), lambda h, i: (h, 0, 0)),
            pl.BlockSpec((1, dh), const2),
            pl.BlockSpec((1, dh), const2),
            pl.BlockSpec((1, dh), const2),
            pl.BlockSpec((1, dh), const2),
            pl.BlockSpec((1, hw), const2),
        ],
        out_specs=pl.BlockSpec((tq, hw), lambda h, i: (i, h)),
        out_shape=jax.ShapeDtypeStruct((s, n_heads * hw), BF16),
        scratch_shapes=[
            pltpu.VMEM((2, s, hw), BF16),
            pltpu.VMEM((s, 2 * hw), BF16),
            pltpu.VMEM((2, tq, V7X_LANES), F32),
            pltpu.VMEM((2, tq, 2 * hw), F32),
            pltpu.VMEM((2, 2, tq, tk), F32),
            pltpu.VMEM((2, 2, tq, V7X_LANES), F32),
        ],
        compiler_params=_cparams(("arbitrary", "arbitrary"), vmem),
        name="diff_attention",
    )(qkv, qkv, qkv, pos, ones, tri_bias, qc, shift, vec(lq1), vec(lk1), vec(lq2), vec(lk2), vec(subln_w))


def _rg_lru_kernel(rx_ref, rg_ref, cw_ref, cb_ref, wa_ref, ba_ref, wx_ref, bx_ref, lam_ref, o_ref,
                   ext_ref, a_ref, b_ref, tail_ref, h_ref, *, t_rows, cw, blk):
    t = pl.program_id(1)
    halo = V7X_SUBLANES

    @pl.when(t == 0)
    def _():
        tail_ref[...] = jnp.zeros(tail_ref.shape, F32)
        h_ref[...] = jnp.zeros(h_ref.shape, F32)

    rx = rx_ref[...]
    ext_ref[0:halo, :] = tail_ref[...]
    ext_ref[halo:, :] = rx
    tail_ref[...] = rx[t_rows - halo:, :]
    ext = ext_ref[...]
    taps = cw_ref[...]
    n_taps = taps.shape[0]
    xr = cb_ref[...]
    for j in range(n_taps):
        lag = n_taps - 1 - j
        xs = rx if lag == 0 else pltpu.roll(ext, lag, 0)[halo:, :]
        xr = xr + xs * taps[j:j + 1, :]

    ra, ia = [], []
    for g in range(cw // blk):
        xb = xr[:, g * blk:(g + 1) * blk].astype(BF16)
        ra.append(jnp.dot(xb, wa_ref[g], preferred_element_type=F32))
        ia.append(jnp.dot(xb, wx_ref[g], preferred_element_type=F32))
    r = jax.nn.sigmoid(jnp.concatenate(ra, axis=-1) + ba_ref[...])
    ig = jax.nn.sigmoid(jnp.concatenate(ia, axis=-1) + bx_ref[...])
    neg_lam = -lam_ref[...]
    softplus = jnp.maximum(neg_lam, 0.0) + jnp.log1p(jnp.exp(-jnp.abs(neg_lam)))
    log_a = (-RG_C) * r * softplus
    th = jnp.tanh(log_a)
    a_ref[...] = jnp.exp(log_a)
    u = -2.0 * th / (1.0 - th)
    b_ref[...] = jnp.where(u > 0.0, u * lax.rsqrt(u), 0.0) * (ig * xr)

    row = lax.broadcasted_iota(jnp.int32, (V7X_SUBLANES, cw), 0)

    def step(n, h):
        rows = pl.ds(pl.multiple_of(n * V7X_SUBLANES, V7X_SUBLANES), V7X_SUBLANES)
        a = a_ref[rows, :]
        b = b_ref[rows, :]
        d = 1
        while d < V7X_SUBLANES:
            keep = row >= d
            a_s = jnp.where(keep, pltpu.roll(a, d, 0), 1.0)
            b_s = jnp.where(keep, pltpu.roll(b, d, 0), 0.0)
            b = a * b_s + b
            a = a * a_s
            d *= 2
        hh = a * h + b
        b_ref[rows, :] = hh
        return hh[V7X_SUBLANES - 1:, :]

    h_last = lax.fori_loop(0, t_rows // V7X_SUBLANES, step, h_ref[0:1, :], unroll=4)
    h_ref[...] = jnp.broadcast_to(h_last, h_ref.shape)
    o_ref[...] = (b_ref[...] * jax.nn.gelu(rg_ref[...])).astype(o_ref.dtype)


def _rg_lru(rest, conv_w, conv_b, w_a, b_a, w_x, b_x, lru_lambda, d_rnn):
    s = rest.shape[0]
    n_blk, blk, _ = w_a.shape
    cw = _tile(d_rnn, 512, blk)
    t_rows = _tile(s, 1024, BF16_SUBLANES)
    n_cb = d_rnn // cw
    row = lambda a: a.reshape(1, -1).astype(F32)
    kern = functools.partial(_rg_lru_kernel, t_rows=t_rows, cw=cw, blk=blk)
    vec_spec = pl.BlockSpec((1, cw), lambda n, t: (0, n))
    gate_spec = pl.BlockSpec((cw // blk, blk, blk), lambda n, t: (n, 0, 0))
    vmem = 2 * (2 * t_rows * cw * 4 + t_rows * cw * 2) + 12 * t_rows * cw * 4 + 4 * MIB
    return pl.pallas_call(
        kern,
        grid=(n_cb, s // t_rows),
        in_specs=[
            pl.BlockSpec((t_rows, cw), lambda n, t: (t, n)),
            pl.BlockSpec((t_rows, cw), lambda n, t: (t, n_cb + n)),
            pl.BlockSpec((conv_w.shape[0], cw), lambda n, t: (0, n)),
            vec_spec, gate_spec, vec_spec, gate_spec, vec_spec, vec_spec,
        ],
        out_specs=pl.BlockSpec((t_rows, cw), lambda n, t: (t, n)),
        out_shape=jax.ShapeDtypeStruct((s, d_rnn), BF16),
        scratch_shapes=[
            pltpu.VMEM((t_rows + V7X_SUBLANES, cw), F32),
            pltpu.VMEM((t_rows, cw), F32),
            pltpu.VMEM((t_rows, cw), F32),
            pltpu.VMEM((V7X_SUBLANES, cw), F32),
            pltpu.VMEM((V7X_SUBLANES, cw), F32),
        ],
        compiler_params=_cparams(("parallel", "arbitrary"), vmem),
        name="rg_lru",
    )(rest, rest, conv_w.astype(F32), row(conv_b), w_a.astype(BF16), row(b_a), w_x.astype(BF16), row(b_x),
      row(lru_lambda))


def _gated_merge_kernel(ya_ref, yr_ref, wa_ref, wr_ref, ga_ref, gb_ref, o_ref, wab_ref, wrb_ref):
    _cast_weight_once(wa_ref, wab_ref)
    _cast_weight_once(wr_ref, wrb_ref)
    pa = jnp.dot(ya_ref[...], wab_ref[...], preferred_element_type=F32)
    pr = jnp.dot(yr_ref[...], wrb_ref[...], preferred_element_type=F32)
    o_ref[...] = (jax.nn.sigmoid(ga_ref[...]) * pa + jax.nn.sigmoid(gb_ref[...]) * pr).astype(o_ref.dtype)


def _gated_merge(y_att, y_rec, w_pa, w_pr, rest, ga_col, gb_col):
    m, ka = y_att.shape
    _, kr = y_rec.shape
    n = w_pa.shape[1]
    tm = _tile(m, 1024, BF16_SUBLANES)
    tn = _tile(math.gcd(math.gcd(n, ga_col), gb_col), 512, V7X_LANES)
    vmem = (2 * (tm * (ka + kr) * 2 + (ka + kr) * tn * 4 + 2 * tm * tn * 4 + tm * tn * 2) + (ka + kr) * tn * 2
            + 4 * tm * tn * 4 + 2 * MIB)
    return pl.pallas_call(
        _gated_merge_kernel,
        grid=(n // tn, m // tm),
        in_specs=[
            pl.BlockSpec((tm, ka), lambda j, i: (i, 0)),
            pl.BlockSpec((tm, kr), lambda j, i: (i, 0)),
            pl.BlockSpec((ka, tn), lambda j, i: (0, j)),
            pl.BlockSpec((kr, tn), lambda j, i: (0, j)),
            pl.BlockSpec((tm, tn), lambda j, i: (i, ga_col // tn + j)),
            pl.BlockSpec((tm, tn), lambda j, i: (i, gb_col // tn + j)),
        ],
        out_specs=pl.BlockSpec((tm, tn), lambda j, i: (i, j)),
        out_shape=jax.ShapeDtypeStruct((m, n), BF16),
        scratch_shapes=[pltpu.VMEM((ka, tn), BF16), pltpu.VMEM((kr, tn), BF16)],
        compiler_params=_cparams(("arbitrary", "arbitrary"), vmem),
        name="gated_merge",
    )(y_att, y_rec, w_pa, w_pr, rest, rest)


def _matmul_residual_kernel(a_ref, w_ref, x_ref, g_ref, o_ref, wb_ref):
    _cast_weight_once(w_ref, wb_ref)
    y = jnp.dot(a_ref[...], wb_ref[...], preferred_element_type=F32)
    o_ref[...] = x_ref[...] + g_ref[...] * y


def _matmul_residual(a, w, x, mod, gate_idx):
    m, k = a.shape
    n = w.shape[1]
    tm = _tile(m, 1024, BF16_SUBLANES)
    tn = _tile(n, 512, V7X_LANES)
    vmem = 2 * (tm * k * 2 + k * tn * 4 + 2 * tm * tn * 4) + k * tn * 2 + tm * tn * 4 + 2 * MIB
    return pl.pallas_call(
        _matmul_residual_kernel,
        grid=(n // tn, m // tm),
        in_specs=[
            pl.BlockSpec((tm, k), lambda j, i: (i, 0)),
            pl.BlockSpec((k, tn), lambda j, i: (0, j)),
            pl.BlockSpec((tm, tn), lambda j, i: (i, j)),
            pl.BlockSpec((1, tn), lambda j, i: (0, gate_idx * (n // tn) + j)),
        ],
        out_specs=pl.BlockSpec((tm, tn), lambda j, i: (i, j)),
        out_shape=jax.ShapeDtypeStruct((m, n), F32),
        scratch_shapes=[pltpu.VMEM((k, tn), BF16)],
        compiler_params=_cparams(("arbitrary", "arbitrary"), vmem),
        name="matmul_residual",
    )(a, w, x, mod)


def _conv_ffn_kernel(h_ref, wg_ref, wu_ref, cw_ref, cb_ref, wd_ref, x_ref, g_ref, nw_ref, o_ref,
                     gate_ref, up_ref, act_ref, carry_ref, *, tm, dn, rn, rc, nf, final_norm):
    mi = pl.program_id(0)
    f = pl.program_id(1)
    halo = V7X_SUBLANES
    d = o_ref.shape[1]

    def gate_up(slot):
        @pl.when(mi == 0)
        def _():
            carry_ref[f] = jnp.zeros(carry_ref.shape[1:], F32)

        h = h_ref[...]
        gate = jnp.dot(h, wg_ref[...], preferred_element_type=F32)
        up_ref[slot] = jnp.dot(h, wu_ref[...], preferred_element_type=F32)
        gate_ref[slot, 0:halo, :] = carry_ref[f]
        gate_ref[slot, halo:, :] = gate
        carry_ref[f] = gate[tm - halo:, :]

    def activation_rows(slot, r):
        r0 = r * rc
        taps = cw_ref[f - 1]
        n_taps = taps.shape[0]
        gc = cb_ref[f - 1]
        for j in range(n_taps):
            start = halo - (n_taps - 1 - j) + r0
            gc = gc + gate_ref[slot, start:start + rc, :] * taps[j:j + 1, :]
        act_ref[slot, r0:r0 + rc, :] = (jax.nn.gelu(gc) * up_ref[slot, r0:r0 + rc, :]).astype(BF16)

    def down_cols(slot, c):
        cols = slice(c * dn, (c + 1) * dn)
        o_ref[:, cols] += jnp.dot(act_ref[slot], wd_ref[:, cols], preferred_element_type=F32)

    def step(gu_slot, act_slot, down_slot):
        n_c = d // dn
        n_r = tm // rc
        if gu_slot is not None:
            gate_up(gu_slot)
        for k in range(max(n_c, n_r)):
            if act_slot is not None and k < n_r:
                activation_rows(act_slot, k)
            if down_slot is not None and k < n_c:
                down_cols(down_slot, k)

    @pl.when(f == 0)
    def _():
        o_ref[...] = jnp.zeros(o_ref.shape, F32)
        step(0, None, None)

    @pl.when(f == 1)
    def _():
        step(1, 0, None)

    @pl.when(jnp.logical_and(jnp.logical_and(f >= 2, f < nf), f % 2 == 0))
    def _():
        step(0, 1, 0)

    @pl.when(jnp.logical_and(jnp.logical_and(f >= 2, f < nf), f % 2 == 1))
    def _():
        step(1, 0, 1)

    @pl.when(f == nf)
    def _():
        step(None, (nf - 1) % 2, nf % 2)

    @pl.when(f == nf + 1)
    def _():
        step(None, None, (nf - 1) % 2)
        for r in range(tm // rn):
            rows = slice(r * rn, (r + 1) * rn)
            x2 = x_ref[rows, :] + g_ref[...] * o_ref[rows, :]
            if final_norm:
                x2 = x2 * _rms_scale(x2) * nw_ref[...]
            o_ref[rows, :] = x2


def _conv_ffn(h, w_up, conv_w, conv_b, w_down, x, mod, gate_idx, norm_w, final_norm):
    s, d = h.shape
    d_ff = w_down.shape[0]
    tm = _tile(s, 512, BF16_SUBLANES)
    tf = _tile(d_ff, 256, V7X_LANES)
    nf = d_ff // tf
    assert nf >= 2
    dn = _tile(d, 512, V7X_LANES)
    rn = _tile(tm, 128, V7X_SUBLANES)
    rc = _tile(tm, 64, BF16_SUBLANES)
    kern = functools.partial(_conv_ffn_kernel, tm=tm, dn=dn, rn=rn, rc=rc, nf=nf, final_norm=final_norm)
    vmem = (2 * (tm * d * 2 + 2 * d * tf * 2 + tf * d * 2 + tm * d * 4) + tm * d * 4
            + 12 * tm * tf * 4 + 4 * rn * d * 4 + tm * dn * 4 + 2 * MIB)
    chunk = lambda f, lag: jnp.clip(f - lag, 0, nf - 1)
    n_taps = conv_w.shape[0]
    taps_by_chunk = conv_w.astype(F32).reshape(n_taps, nf, tf).transpose(1, 0, 2)
    bias_by_chunk = conv_b.astype(F32).reshape(nf, 1, tf)
    return pl.pallas_call(
        kern,
        grid=(s // tm, nf + 2),
        in_specs=[
            pl.BlockSpec((tm, d), lambda i, f: (i, 0), pipeline_mode=pl.Buffered(1)),
            pl.BlockSpec((d, tf), lambda i, f: (0, chunk(f, 0))),
            pl.BlockSpec((d, tf), lambda i, f: (0, nf + chunk(f, 0))),
            pl.BlockSpec((nf, n_taps, tf), lambda i, f: (0, 0, 0)),
            pl.BlockSpec((nf, 1, tf), lambda i, f: (0, 0, 0)),
            pl.BlockSpec((tf, d), lambda i, f: (chunk(f, 2), 0)),
            pl.BlockSpec((tm, d), lambda i, f: (i, 0)),
            pl.BlockSpec((1, d), lambda i, f: (0, gate_idx)),
            pl.BlockSpec((1, d), lambda i, f: (0, 0)),
        ],
        out_specs=pl.BlockSpec((tm, d), lambda i, f: (i, 0)),
        out_shape=jax.ShapeDtypeStruct((s, d), F32),
        scratch_shapes=[
            pltpu.VMEM((2, tm + V7X_SUBLANES, tf), F32),
            pltpu.VMEM((2, tm, tf), F32),
            pltpu.VMEM((2, tm, tf), BF16),
            pltpu.VMEM((nf, V7X_SUBLANES, tf), F32),
        ],
        compiler_params=_cparams(("arbitrary", "arbitrary"), vmem),
        name="conv_ffn",
    )(h, w_up, w_up, taps_by_chunk, bias_by_chunk, w_down, x, mod, norm_w.reshape(1, -1).astype(F32))


def kernel(x, c, w_ada, b_ada, norm1_w, w_in, lambda_q1, lambda_k1, lambda_q2, lambda_k2, subln_w, conv_w, conv_b, w_rg_a, b_rg_a, w_rg_x, b_rg_x, lru_lambda, w_proj_attn, w_proj_rec, w_out, norm2_w, w_ffn_up, ffn_conv_w, ffn_conv_b, w_ffn_down, norm_f_w):
    batch, seq, d = x.shape
    depth = w_ada.shape[0]
    dh = lambda_q1.shape[-1]
    attn_w = w_proj_attn.shape[1]
    n_heads = attn_w // (2 * dh)
    d_rnn = conv_w.shape[-1]
    assert w_in.shape[-1] == 3 * attn_w + 2 * d_rnn + 2 * d
    row = lambda a: a.reshape(1, -1).astype(F32)

    outs = []
    for b in range(batch):
        xb = x[b]
        cb = c[b:b + 1]
        for l in range(depth):
            lambda_init = 0.8 - 0.6 * math.exp(-0.3 * l)
            mod = _adaln_mod(cb, w_ada[l], row(b_ada[l]))
            h1 = _norm_modulate(xb, row(norm1_w[l]), mod, 0, 1)
            q_scale = jnp.concatenate([jnp.full((1, attn_w), dh ** -0.5 * LOG2E, F32),
                                       jnp.ones((1, 2 * attn_w), F32)], axis=1)
            qkv = _matmul(h1, w_in[l], 0, 3 * attn_w, BF16, q_scale)
            rest = _matmul(h1, w_in[l], 3 * attn_w, 2 * d_rnn + 2 * d, F32)
            y_att = _diff_attention(qkv, lambda_q1[l], lambda_k1[l], lambda_q2[l], lambda_k2[l], subln_w[l],
                                    lambda_init, n_heads, dh)
            y_rec = _rg_lru(rest, conv_w[l], conv_b[l], w_rg_a[l], b_rg_a[l], w_rg_x[l], b_rg_x[l],
                            lru_lambda[l], d_rnn)
            mixed = _gated_merge(y_att, y_rec, w_proj_attn[l], w_proj_rec[l], rest, 2 * d_rnn, 2 * d_rnn + d)
            x1 = _matmul_residual(mixed, w_out[l], xb, mod, 2)
            h2 = _norm_modulate(x1, row(norm2_w[l]), mod, 3, 4)
            last = l == depth - 1
            xb = _conv_ffn(h2, w_ffn_up[l].astype(BF16), ffn_conv_w[l], ffn_conv_b[l], w_ffn_down[l].astype(BF16),
                           x1, mod, 5, norm_f_w, final_norm=last)
        outs.append(xb)
    return outs[0][None] if batch == 1 else jnp.stack(outs, axis=0)
```
